```python
import math
import jax
import jax.numpy as jnp
from jax import lax
import numpy as np

D_MODEL = 4096
BATCH = 2
SEQ = 8192
DEPTH = 1

CHUNK = 64
Q_BLOCK = 128
HEAD_DIM = 128
SB_HEADS = D_MODEL // (2 * HEAD_DIM)
DN_HEADS = D_MODEL // (2 * HEAD_DIM)
MEM_HEADS = 4
N_MEM = 256
CONV_K = 4
N_BRANCH = 3
SB_WIDTH = SB_HEADS * HEAD_DIM
DN_WIDTH = DN_HEADS * HEAD_DIM
MEM_WIDTH = MEM_HEADS * HEAD_DIM
IN_SIZES = (SB_WIDTH,) * 4 + (DN_WIDTH,) * 4 + (DN_HEADS,) * 2 + (MEM_WIDTH,) * 2
IN_COLS = sum(IN_SIZES)
LN_EPS = 1e-5
RMS_EPS = 1e-6
L2_EPS = 1e-6
DEEPNORM_ALPHA = (2.0 * DEPTH) ** 0.25
DEEPNORM_BETA = (8.0 * DEPTH) ** -0.25

kernel_name = 'hybrid_stickbreak_deltanet_memxattn_block'


def _split_heads(t, n_heads):
    b, s, _ = t.shape
    return t.reshape(b, s, n_heads, HEAD_DIM).transpose(0, 2, 1, 3)


def _merge_heads(t):
    b, h, s, d = t.shape
    return t.transpose(0, 2, 1, 3).reshape(b, s, h * d)


def _l2norm(t):
    t = t.astype(jnp.float32)
    return t * lax.rsqrt(jnp.sum(t * t, axis=-1, keepdims=True) + L2_EPS)


def _layer_norm(x, g, b):
    xf = x.astype(jnp.float32)
    mu = jnp.mean(xf, axis=-1, keepdims=True)
    xc = xf - mu
    var = jnp.mean(xc * xc, axis=-1, keepdims=True)
    return (xc * lax.rsqrt(var + LN_EPS) * g + b).astype(x.dtype)


def _causal_depthwise_conv(x, w):
    k, c = w.shape
    return lax.conv_general_dilated(
        x, w[:, None, :], window_strides=(1,), padding=((k - 1, 0),),
        dimension_numbers=('NWC', 'WIO', 'NWC'), feature_group_count=c)


def stick_breaking_attention(q, k, v):
    b, h, s, dh = q.shape
    nb = s // Q_BLOCK
    scale = dh ** -0.5
    q_blocks = jnp.moveaxis(q.reshape(b, h, nb, Q_BLOCK, dh), 2, 0)
    starts = jnp.arange(nb, dtype=jnp.int32) * Q_BLOCK
    key_pos = jnp.arange(s, dtype=jnp.int32)

    def block(args):
        q_blk, start = args
        z = jnp.einsum('bhqd,bhkd->bhqk', q_blk, k).astype(jnp.float32) * scale
        t = start + jnp.arange(Q_BLOCK, dtype=jnp.int32)
        before = key_pos[None, :] < t[:, None]
        sp = jnp.where(before, jax.nn.softplus(z), 0.0)
        suffix = lax.cumsum(sp, axis=3, reverse=True) - sp
        w = jnp.where(before, jnp.exp(-jax.nn.softplus(-z) - suffix), 0.0)
        return jnp.einsum('bhqk,bhkd->bhqd', w.astype(v.dtype), v)

    out = lax.map(block, (q_blocks, starts))
    return jnp.moveaxis(out, 0, 2).reshape(b, h, s, dh)


def gated_delta_rule(q, k, v, g, beta):
    b, h, s, dk = q.shape
    dv = v.shape[-1]
    n = s // CHUNK
    f32 = jnp.float32
    q, k, v = (t.astype(f32).reshape(b, h, n, CHUNK, t.shape[-1]) for t in (q, k, v))
    g = lax.cumsum(g.astype(f32).reshape(b, h, n, CHUNK), axis=3)
    beta = beta.astype(f32).reshape(b, h, n, CHUNK)
    idx = jnp.arange(CHUNK)
    incl = idx[:, None] >= idx[None, :]
    strict = idx[:, None] > idx[None, :]
    diff = g[..., :, None] - g[..., None, :]
    decay = jnp.where(incl, jnp.exp(jnp.where(incl, diff, 0.0)), 0.0)
    k_beta = k * beta[..., None]
    kk = jnp.einsum('bhnid,bhnjd->bhnij', k_beta, k)
    tri = jnp.eye(CHUNK, dtype=f32) + jnp.where(strict, kk * decay, 0.0)
    rhs = jnp.concatenate([v * beta[..., None], k_beta * jnp.exp(g)[..., None]], axis=-1)
    sol = lax.linalg.triangular_solve(tri, rhs, left_side=True, lower=True, unit_diagonal=True)
    u, w = sol[..., :dv], sol[..., dv:]
    a_intra = jnp.einsum('bhnid,bhnjd->bhnij', q, k) * decay
    g_last = g[..., -1]
    k_end = k * jnp.exp(g_last[..., None] - g)[..., None]
    q_dec = q * jnp.exp(g)[..., None]

    def step(state, inp):
        q_c, u_c, w_c, a_c, k_c, gl_c = inp
        v_new = u_c - jnp.einsum('bhck,bhkv->bhcv', w_c, state)
        o_c = jnp.einsum('bhck,bhkv->bhcv', q_c, state) + jnp.einsum('bhij,bhjv->bhiv', a_c, v_new)
        state = state * jnp.exp(gl_c)[..., None, None] + jnp.einsum('bhck,bhcv->bhkv', k_c, v_new)
        return state, o_c

    xs = tuple(jnp.moveaxis(t, 2, 0) for t in (q_dec, u, w, a_intra, k_end, g_last))
    state0 = jnp.zeros((b, h, dk, dv), f32)
    _, o = lax.scan(step, state0, xs)
    return jnp.moveaxis(o, 0, 2).reshape(b, h, s, dv)


def hybrid_layer(x, mem, w_in, conv_w, a_log, dt_bias, dn_norm_w, w_mem_kv, w_gate, b_gate,
                 w_up_sb, w_up_dn, w_up_mem, w_out, ln_g, ln_b):
    f32 = jnp.float32
    h = jnp.einsum('bsd,de->bse', x, w_in)
    splits = np.cumsum(IN_SIZES)[:-1].tolist()
    (q_sb, k_sb, v_sb, z_sb, q_dn, k_dn, v_dn, z_dn, b_dn, a_dn, q_m, z_m) = jnp.split(h, splits, axis=-1)

    o_sb = stick_breaking_attention(_split_heads(q_sb, SB_HEADS), _split_heads(k_sb, SB_HEADS),
                                    _split_heads(v_sb, SB_HEADS))
    o_sb = _merge_heads(o_sb) * jax.nn.silu(z_sb)

    qkv = jax.nn.silu(_causal_depthwise_conv(jnp.concatenate([q_dn, k_dn, v_dn], axis=-1), conv_w))
    q_dn, k_dn, v_dn = jnp.split(qkv, 3, axis=-1)
    q_dn = _l2norm(_split_heads(q_dn, DN_HEADS)) * (HEAD_DIM ** -0.5)
    k_dn = _l2norm(_split_heads(k_dn, DN_HEADS))
    beta = jax.nn.sigmoid(b_dn.astype(f32)).transpose(0, 2, 1)
    g = (-jnp.exp(a_log.astype(f32)) * jax.nn.softplus(a_dn.astype(f32) + dt_bias.astype(f32))).transpose(0, 2, 1)
    o_dn = gated_delta_rule(q_dn, k_dn, _split_heads(v_dn, DN_HEADS), g, beta)
    o_dn = o_dn * lax.rsqrt(jnp.mean(o_dn * o_dn, axis=-1, keepdims=True) + RMS_EPS) * dn_norm_w
    o_dn = _merge_heads(o_dn).astype(x.dtype) * jax.nn.silu(z_dn)

    k_m, v_m = jnp.split(jnp.einsum('bnd,de->bne', mem, w_mem_kv), 2, axis=-1)
    scores = jnp.einsum('bhsd,bhnd->bhsn', _split_heads(q_m, MEM_HEADS),
                        _split_heads(k_m, MEM_HEADS)).astype(f32) * (HEAD_DIM ** -0.5)
    p = jax.nn.softmax(scores, axis=-1).astype(v_m.dtype)
    o_m = _merge_heads(jnp.einsum('bhsn,bhnd->bhsd', p, _split_heads(v_m, MEM_HEADS))) * jax.nn.silu(z_m)

    y = (jax.nn.sigmoid(jnp.einsum('bsd,de->bse', x, w_gate[0]) + b_gate[0]) * jnp.einsum('bsc,cd->bsd', o_sb, w_up_sb)
         + jax.nn.sigmoid(jnp.einsum('bsd,de->bse', x, w_gate[1]) + b_gate[1]) * jnp.einsum('bsc,cd->bsd', o_dn, w_up_dn)
         + jax.nn.sigmoid(jnp.einsum('bsd,de->bse', x, w_gate[2]) + b_gate[2]) * jnp.einsum('bsc,cd->bsd', o_m, w_up_mem))
    out = jnp.einsum('bsd,de->bse', y, w_out)

    return _layer_norm(DEEPNORM_ALPHA * x + out, ln_g, ln_b)


def setup_inputs(seed: int = 0) -> dict:
    key = jax.random.key(seed)
    ks = jax.random.split(key, 16)
    f32 = jnp.float32

    def dense(k, fan_in, shape, scale=1.0):
        return jax.random.normal(k, shape, f32) * (scale * fan_in ** -0.5)

    x = jax.random.normal(ks[0], (BATCH, SEQ, D_MODEL), f32)
    mem = jax.random.normal(ks[1], (BATCH, N_MEM, D_MODEL), f32)
    w_in = dense(ks[2], D_MODEL, (DEPTH, D_MODEL, IN_COLS))
    conv_w = dense(ks[3], CONV_K, (DEPTH, CONV_K, 3 * DN_WIDTH))
    a_log = jnp.log(jax.random.uniform(ks[4], (DEPTH, DN_HEADS), f32, 1.0, 16.0))
    dt = jnp.exp(jax.random.uniform(ks[5], (DEPTH, DN_HEADS), f32, math.log(1e-3), math.log(1e-1)))
    dt_bias = dt + jnp.log(-jnp.expm1(-dt))
    dn_norm_w = 1.0 + 0.02 * jax.random.normal(ks[6], (DEPTH, HEAD_DIM), f32)
    w_mem_kv = dense(ks[7], D_MODEL, (DEPTH, D_MODEL, 2 * MEM_WIDTH))
    w_gate = dense(ks[8], D_MODEL, (DEPTH, N_BRANCH, D_MODEL, D_MODEL))
    b_gate = 0.02 * jax.random.normal(ks[9], (DEPTH, N_BRANCH, D_MODEL), f32)
    w_up_sb = dense(ks[10], SB_WIDTH, (DEPTH, SB_WIDTH, D_MODEL), DEEPNORM_BETA)
    w_up_dn = dense(ks[11], DN_WIDTH, (DEPTH, DN_WIDTH, D_MODEL), DEEPNORM_BETA)
    w_up_mem = dense(ks[12], MEM_WIDTH, (DEPTH, MEM_WIDTH, D_MODEL), DEEPNORM_BETA)
    w_out = dense(ks[13], D_MODEL, (DEPTH, D_MODEL, D_MODEL), DEEPNORM_BETA)
    ln_g = 1.0 + 0.02 * jax.random.normal(ks[14], (DEPTH, D_MODEL), f32)
    ln_b = 0.02 * jax.random.normal(ks[15], (DEPTH, D_MODEL), f32)
    return {'x': x, 'mem': mem, 'w_in': w_in, 'conv_w': conv_w, 'a_log': a_log, 'dt_bias': dt_bias,
            'dn_norm_w': dn_norm_w, 'w_mem_kv': w_mem_kv, 'w_gate': w_gate, 'b_gate': b_gate,
            'w_up_sb': w_up_sb, 'w_up_dn': w_up_dn, 'w_up_mem': w_up_mem, 'w_out': w_out,
            'ln_g': ln_g, 'ln_b': ln_b}


def reference(x, mem, w_in, conv_w, a_log, dt_bias, dn_norm_w, w_mem_kv, w_gate, b_gate,
              w_up_sb, w_up_dn, w_up_mem, w_out, ln_g, ln_b):
    for l in range(DEPTH):
        x = hybrid_layer(x, mem, w_in[l], conv_w[l], a_log[l], dt_bias[l], dn_norm_w[l], w_mem_kv[l],
                         w_gate[l], b_gate[l], w_up_sb[l], w_up_dn[l], w_up_mem[l], w_out[l],
                         ln_g[l], ln_b[l])
    return x
```

```python
import functools

import numpy as np
import jax
import jax.numpy as jnp
from jax import lax
from jax.experimental import pallas as pl
from jax.experimental.pallas import tpu as pltpu

HEAD_DIM = 128
CONV_K = 4
MEM_HEADS = 4
LN_EPS = 1e-5
RMS_EPS = 1e-6
L2_EPS = 1e-6

DN_CHUNK = 128
DN_GROUP = 4
SB_BLOCK = 256
V7X_VMEM_LIMIT_BYTES = 48 * 1024 * 1024

F32 = jnp.float32
BF16 = jnp.bfloat16


def _params(n_axes):
    return pltpu.CompilerParams(dimension_semantics=("arbitrary",) * n_axes,
                                vmem_limit_bytes=V7X_VMEM_LIMIT_BYTES)


def _dot(a, b):
    return jnp.dot(a, b, preferred_element_type=F32)


def _dot_nt(a, b):
    return lax.dot_general(a, b, (((1,), (1,)), ((), ())), preferred_element_type=F32)


def _sigmoid(x):
    return 1.0 / (1.0 + jnp.exp(-x))


def _softplus(x):
    return jnp.maximum(x, 0.0) + jnp.log(1.0 + jnp.exp(-jnp.abs(x)))


def _tile(n, pref):
    t = min(n, pref)
    assert n % t == 0, (n, t)
    return t


def _tiled_matmul(body, a, b, extra, n_extra_out_cols, out_dtype, tm, tn, name):
    m, k = a.shape
    n = b.shape[1]
    assert m % tm == 0 and n % tn == 0
    grid = (n // tn, m // tm)
    in_specs = [pl.BlockSpec((tm, k), lambda j, i: (i, 0)),
                pl.BlockSpec((k, tn), lambda j, i: (0, j))]
    in_specs += [pl.BlockSpec(blk, imap) for _, blk, imap in extra]
    return pl.pallas_call(
        body,
        grid=grid,
        in_specs=in_specs,
        out_specs=pl.BlockSpec((tm, tn), lambda j, i: (i, j)),
        out_shape=jax.ShapeDtypeStruct((m, n), out_dtype),
        compiler_params=_params(2),
        name=name,
    )(a, b, *[e[0] for e in extra])


def _proj_scale_body(a_ref, b_ref, s_ref, o_ref):
    o_ref[...] = (_dot(a_ref[...], b_ref[...]) * s_ref[...]).astype(o_ref.dtype)


def _proj_body(a_ref, b_ref, o_ref):
    o_ref[...] = _dot(a_ref[...], b_ref[...]).astype(o_ref.dtype)


def _gate_body(a_ref, b_ref, bias_ref, o_ref):
    o_ref[...] = _sigmoid(_dot(a_ref[...], b_ref[...]) + bias_ref[...]).astype(o_ref.dtype)


def _resid_body(a_ref, b_ref, x_ref, o_ref, *, alpha):
    o_ref[...] = alpha * x_ref[...] + _dot(a_ref[...], b_ref[...])


def _merge_body(osb_ref, odn_ref, om_ref, wsb_ref, wdn_ref, wm_ref, g0_ref, g1_ref, g2_ref, o_ref):
    y = g0_ref[...].astype(F32) * _dot(osb_ref[...], wsb_ref[...])
    y += g1_ref[...].astype(F32) * _dot(odn_ref[...], wdn_ref[...])
    y += g2_ref[...].astype(F32) * _dot(om_ref[...], wm_ref[...])
    o_ref[...] = y.astype(o_ref.dtype)


def _merge(o_sb, o_dn, o_m, w_sb, w_dn, w_m, gates, tm, tn):
    m = o_sb.shape[0]
    d = w_sb.shape[1]
    nb = d // tn
    row = lambda w: pl.BlockSpec((tm, w), lambda j, i: (i, 0))
    colw = lambda w: pl.BlockSpec((w, tn), lambda j, i: (0, j))
    gate = lambda g: pl.BlockSpec((tm, tn), lambda j, i: (i, g * nb + j))
    return pl.pallas_call(
        _merge_body,
        grid=(nb, m // tm),
        in_specs=[row(o_sb.shape[1]), row(o_dn.shape[1]), row(o_m.shape[1]),
                  colw(w_sb.shape[0]), colw(w_dn.shape[0]), colw(w_m.shape[0]),
                  gate(0), gate(1), gate(2)],
        out_specs=pl.BlockSpec((tm, tn), lambda j, i: (i, j)),
        out_shape=jax.ShapeDtypeStruct((m, d), BF16),
        compiler_params=_params(2),
        name="branch_merge",
    )(o_sb, o_dn, o_m, w_sb, w_dn, w_m, gates, gates, gates)


def _layernorm_body(r_ref, g_ref, b_ref, o_ref):
    r = r_ref[...]
    mu = jnp.mean(r, axis=-1, keepdims=True)
    rc = r - mu
    var = jnp.mean(rc * rc, axis=-1, keepdims=True)
    o_ref[...] = rc * lax.rsqrt(var + LN_EPS) * g_ref[...] + b_ref[...]


def _layernorm(r, g, b, tm):
    m, d = r.shape
    return pl.pallas_call(
        _layernorm_body,
        grid=(m // tm,),
        in_specs=[pl.BlockSpec((tm, d), lambda i: (i, 0)),
                  pl.BlockSpec((1, d), lambda i: (0, 0)),
                  pl.BlockSpec((1, d), lambda i: (0, 0))],
        out_specs=pl.BlockSpec((tm, d), lambda i: (i, 0)),
        out_shape=jax.ShapeDtypeStruct((m, d), F32),
        compiler_params=_params(1),
        name="post_layernorm",
    )(r, g.reshape(1, d), b.reshape(1, d))


def _sb_body(q_ref, k_ref, v_ref, z_ref, tri_ref, o_ref, vt_ref, acc_ref, *, tk, nq):
    qi = pl.program_id(2)

    @pl.when(qi == 0)
    def _():
        def transpose_block(c, _):
            blk = v_ref[pl.ds(pl.multiple_of(c * tk, tk), tk), :]
            vt_ref[c] = blk.astype(F32).T.astype(BF16)
            return 0
        lax.fori_loop(0, nq, transpose_block, 0)

    q = q_ref[...]
    tri = tri_ref[...]
    key_pos = lax.broadcasted_iota(jnp.int32, (tk, tk), 0)
    qry_pos = lax.broadcasted_iota(jnp.int32, (tk, tk), 1)
    before = key_pos < qry_pos

    def block(j, carry, diagonal):
        kb = k_ref[pl.ds(pl.multiple_of(j * tk, tk), tk), :]
        zt = _dot_nt(kb, q)
        sp = _softplus(zt)
        spm = jnp.where(before, sp, 0.0) if diagonal else sp
        hi = spm.astype(BF16)
        lo = (spm - hi.astype(F32)).astype(BF16)
        suffix = _dot(tri, hi) + _dot(tri, lo)
        logw = zt - sp - suffix - carry
        w = jnp.exp(logw)
        if diagonal:
            w = jnp.where(before, w, 0.0)
        pv = _dot(vt_ref[j], w.astype(BF16))
        if diagonal:
            acc_ref[...] = pv
        else:
            acc_ref[...] += pv
        return carry + suffix[0:1, :] + spm[0:1, :]

    carry = block(qi, jnp.zeros((1, tk), F32), True)
    lax.fori_loop(0, qi, lambda s, c: block(qi - 1 - s, c, False), carry)

    z = z_ref[...].astype(F32)
    o_ref[...] = (acc_ref[...].T * (z * _sigmoid(z))).astype(o_ref.dtype)


def _sb_attention(h, batch, seq, heads, col0):
    t = _tile(seq, SB_BLOCK)
    nq = seq // t
    tri = jnp.asarray(np.triu(np.ones((t, t), np.float32), 1), BF16)
    body = functools.partial(_sb_body, tk=t, nq=nq)
    return pl.pallas_call(
        body,
        grid=(batch, heads, nq),
        in_specs=[pl.BlockSpec((t, HEAD_DIM), lambda b, h_, i: (b * nq + i, col0 + h_)),
                  pl.BlockSpec((seq, HEAD_DIM), lambda b, h_, i: (b, col0 + heads + h_)),
                  pl.BlockSpec((seq, HEAD_DIM), lambda b, h_, i: (b, col0 + 2 * heads + h_)),
                  pl.BlockSpec((t, HEAD_DIM), lambda b, h_, i: (b * nq + i, col0 + 3 * heads + h_)),
                  pl.BlockSpec((t, t), lambda b, h_, i: (0, 0))],
        out_specs=pl.BlockSpec((t, HEAD_DIM), lambda b, h_, i: (b * nq + i, h_)),
        out_shape=jax.ShapeDtypeStruct((batch * seq, heads * HEAD_DIM), BF16),
        scratch_shapes=[pltpu.VMEM((nq, HEAD_DIM, t), BF16), pltpu.VMEM((HEAD_DIM, t), F32)],
        compiler_params=_params(3),
        name="stickbreak_attention",
    )(h, h, h, h, tri)


_M_INCL, _M_STRICT, _M_PAIR, _M_LEVEL0 = 0, 1, 2, 3
_N_LEVELS = int(np.log2(DN_CHUNK)) - 1
_M_EYE = _M_LEVEL0 + _N_LEVELS
_M_UPPER = _M_EYE + 1
_N_MASKS = _M_UPPER + 1


def _dn_masks():
    c = DN_CHUNK
    i = np.arange(c)[:, None]
    j = np.arange(c)[None, :]
    masks = [i >= j, i > j, (i % 2 == 1) & (j == i - 1)]
    b = 4
    while b <= c:
        masks.append((i // b == j // b) & (i % b >= b // 2) & (j % b < b // 2))
        b *= 2
    masks += [i == j, i <= j]
    assert len(masks) == _N_MASKS
    return np.stack(masks).astype(np.float32)


def _split3(x):
    p1 = x.astype(BF16)
    r = x - p1.astype(F32)
    p2 = r.astype(BF16)
    p3 = (r - p2.astype(F32)).astype(BF16)
    return p1, p2, p3


def _dn_body(q_ref, k_ref, v_ref, z_ref, cq_ref, ck_ref, cv_ref, b_ref, a_ref, alog_ref, dtb_ref, nw_ref,
             msk_ref, o_ref, gc_ref, beta_ref, gl_ref, *, seq):
    c = DN_CHUNK
    rows = DN_GROUP * c
    n_groups = seq // rows

    g = -jnp.exp(alog_ref[0]) * _softplus(a_ref[0, 0] + dtb_ref[0])
    upper = msk_ref[_M_UPPER].astype(BF16)
    ones = jnp.ones((c, c), BF16)
    pieces = _split3(g)
    gc_ref[...] = sum(_dot(p, upper) for p in pieces)
    gl_ref[...] = sum(_dot(p, ones) for p in pieces)
    beta_ref[...] = _sigmoid(b_ref[0, 0])

    row8 = lax.broadcasted_iota(jnp.int32, (8, HEAD_DIM), 0)

    def conv_silu(x_ref, cw_ref, it, r0):
        x = x_ref[pl.ds(r0, rows), :].astype(F32)
        h0 = jnp.maximum(r0 - 8, 0)
        halo = x_ref[pl.ds(pl.multiple_of(h0, 8), 8), :].astype(F32) * (it > 0).astype(F32)
        cw = cw_ref[...]
        y = x * cw[CONV_K - 1:CONV_K, :]
        for s in range(1, CONV_K):
            xs = pltpu.roll(x, s, axis=0)
            top = jnp.where(row8 < s, pltpu.roll(halo, s, axis=0), xs[0:8, :])
            xs = jnp.concatenate([top, xs[8:, :]], axis=0)
            y += xs * cw[CONV_K - 1 - s:CONV_K - s, :]
        return y * _sigmoid(y)

    def l2norm(t):
        return t * lax.rsqrt(jnp.sum(t * t, axis=-1, keepdims=True) + L2_EPS)

    def group(it, state):
        r0 = pl.multiple_of(it * rows, rows)
        qg = l2norm(conv_silu(q_ref, cq_ref, it, r0)) * (HEAD_DIM ** -0.5)
        kg = l2norm(conv_silu(k_ref, ck_ref, it, r0))
        vg = conv_silu(v_ref, cv_ref, it, r0)

        parts = []
        for ci in range(DN_GROUP):
            sl = slice(ci * c, (ci + 1) * c)
            q, k, v = qg[sl], kg[sl], vg[sl]
            chunk = it * DN_GROUP + ci
            gc_row = jnp.broadcast_to(gc_ref[pl.ds(chunk, 1), :], (c, c))
            gc_col = gc_row.T
            beta_col = jnp.broadcast_to(beta_ref[pl.ds(chunk, 1), :], (c, c)).T
            gl = gl_ref[pl.ds(chunk, 1), :]
            incl = msk_ref[_M_INCL]
            decay = jnp.exp((gc_col - gc_row) * incl) * incl
            eg_col = jnp.exp(gc_col)
            kb = k * beta_col
            kbf = k.astype(BF16)
            prod = _dot_nt(jnp.concatenate([q, kb], axis=0).astype(BF16), kbf)
            a_in = prod[:c] * decay
            a_lo = prod[c:] * decay * msk_ref[_M_STRICT]
            t_inv = msk_ref[_M_EYE] - a_lo * msk_ref[_M_PAIR]
            for lvl in range(_N_LEVELS):
                off = (a_lo * msk_ref[_M_LEVEL0 + lvl]).astype(BF16)
                tb = t_inv.astype(BF16)
                t_inv = t_inv - _dot(_dot(tb, off).astype(BF16), tb)
            rhs = jnp.concatenate([kb * eg_col, v * beta_col], axis=1).astype(BF16)
            wu = _dot(t_inv.astype(BF16), rhs).astype(BF16)
            k_end = k * jnp.exp(gl - gc_col)
            pn = _dot(k_end.T.astype(BF16), wu)
            aw_au = _dot(a_in.astype(BF16), wu)
            q_eff = q * eg_col - aw_au[:, :HEAD_DIM]
            parts.append((q_eff, pn[:, :HEAD_DIM], pn[:, HEAD_DIM:], aw_au[:, HEAD_DIM:], gl))

        for ci, (q_eff, p_mat, n_mat, o_loc, gl) in enumerate(parts):
            both = _dot(jnp.concatenate([q_eff, p_mat], axis=0).astype(BF16), state.astype(BF16))
            o = both[:c] + o_loc
            state = state * jnp.exp(gl) - both[c:] + n_mat
            o = o * lax.rsqrt(jnp.mean(o * o, axis=-1, keepdims=True) + RMS_EPS) * nw_ref[...]
            zr = z_ref[pl.ds(r0 + ci * c, c), :].astype(F32)
            o_ref[pl.ds(r0 + ci * c, c), :] = (o * (zr * _sigmoid(zr))).astype(o_ref.dtype)
        return state

    lax.fori_loop(0, n_groups, group, jnp.zeros((HEAD_DIM, HEAD_DIM), F32))


def _deltanet(h, conv_w, ba, a_log, dt_bias, norm_w, batch, seq, heads, col0):
    c = DN_CHUNK
    assert seq % (DN_GROUP * c) == 0
    n_chunks = seq // c
    masks = jnp.asarray(_dn_masks())
    lane = lambda v: jnp.broadcast_to(v.astype(F32)[:, None, None], (heads, 1, c))
    body = functools.partial(_dn_body, seq=seq)
    hcol = lambda g: pl.BlockSpec((seq, HEAD_DIM), lambda b, h_: (b, col0 + g * heads + h_))
    cw = lambda g: pl.BlockSpec((CONV_K, HEAD_DIM), lambda b, h_: (0, g * heads + h_))
    return pl.pallas_call(
        body,
        grid=(batch, heads),
        in_specs=[hcol(0), hcol(1), hcol(2), hcol(3), cw(0), cw(1), cw(2),
                  pl.BlockSpec((1, 1, n_chunks, c), lambda b, h_: (b, h_, 0, 0)),
                  pl.BlockSpec((1, 1, n_chunks, c), lambda b, h_: (b, heads + h_, 0, 0)),
                  pl.BlockSpec((1, 1, c), lambda b, h_: (h_, 0, 0)),
                  pl.BlockSpec((1, 1, c), lambda b, h_: (h_, 0, 0)),
                  pl.BlockSpec((1, HEAD_DIM), lambda b, h_: (0, 0)),
                  pl.BlockSpec((_N_MASKS, c, c), lambda b, h_: (0, 0, 0))],
        out_specs=pl.BlockSpec((seq, HEAD_DIM), lambda b, h_: (b, h_)),
        out_shape=jax.ShapeDtypeStruct((batch * seq, heads * HEAD_DIM), BF16),
        scratch_shapes=[pltpu.VMEM((n_chunks, c), F32)] * 3,
        compiler_params=_params(2),
        name="gated_deltanet",
    )(h, h, h, h, conv_w, conv_w, conv_w, ba, ba, lane(a_log), lane(dt_bias), norm_w.reshape(1, HEAD_DIM), masks)


def _mem_body(q_ref, z_ref, kv_ref, o_ref, *, width):
    for hh in range(width // HEAD_DIM):
        sl = slice(hh * HEAD_DIM, (hh + 1) * HEAD_DIM)
        q = (q_ref[:, sl] * (HEAD_DIM ** -0.5)).astype(BF16)
        k = kv_ref[:, sl]
        v = kv_ref[:, width + hh * HEAD_DIM:width + (hh + 1) * HEAD_DIM]
        s = _dot_nt(q, k)
        p = jnp.exp(s - jnp.max(s, axis=-1, keepdims=True))
        o = _dot(p.astype(BF16), v) * (1.0 / jnp.sum(p, axis=-1, keepdims=True))
        z = z_ref[:, sl]
        o_ref[:, sl] = (o * (z * _sigmoid(z))).astype(o_ref.dtype)


def _mem_attention(h_tail, kv, batch, seq, n_mem, width, tq):
    nq = seq // tq
    body = functools.partial(_mem_body, width=width)
    return pl.pallas_call(
        body,
        grid=(batch, nq),
        in_specs=[pl.BlockSpec((tq, width), lambda b, i: (b * nq + i, 0)),
                  pl.BlockSpec((tq, width), lambda b, i: (b * nq + i, 1)),
                  pl.BlockSpec((n_mem, 2 * width), lambda b, i: (b, 0))],
        out_specs=pl.BlockSpec((tq, width), lambda b, i: (b * nq + i, 0)),
        out_shape=jax.ShapeDtypeStruct((batch * seq, width), BF16),
        compiler_params=_params(2),
        name="memory_attention",
    )(h_tail, h_tail, kv)


def _layer(x, mem, w_in, conv_w, a_log, dt_bias, dn_norm_w, w_mem_kv, w_gate, b_gate,
           w_up_sb, w_up_dn, w_up_mem, w_out, ln_g, ln_b, alpha):
    batch, seq, d = x.shape
    n_mem = mem.shape[1]
    m = batch * seq
    sbw, dnw, memw = w_up_sb.shape[0], w_up_dn.shape[0], w_up_mem.shape[0]
    sbh, dnh = sbw // HEAD_DIM, dnw // HEAD_DIM
    n_main = 4 * sbw + 4 * dnw
    ba0 = n_main
    qm0 = ba0 + 2 * dnh
    assert w_in.shape[1] == qm0 + 2 * memw

    x2 = x.reshape(m, d)
    xb = x2.astype(BF16)
    tm = _tile(m, 1024)

    w_main = w_in[:, :n_main].astype(BF16)
    col_scale = jnp.concatenate([jnp.full((sbw,), HEAD_DIM ** -0.5, F32), jnp.ones((n_main - sbw,), F32)])
    h = _tiled_matmul(_proj_scale_body, xb, w_main,
                      [(col_scale.reshape(1, n_main), (1, _tile(n_main, 1024)), lambda j, i: (0, j))],
                      0, BF16, tm, _tile(n_main, 1024), "input_projection")
    pad = (-2 * dnh) % HEAD_DIM
    w_tail = jnp.concatenate([w_in[:, qm0:], w_in[:, ba0:qm0], jnp.zeros((d, pad), F32)], axis=1).astype(BF16)
    h_tail = _tiled_matmul(_proj_body, xb, w_tail, [], 0, F32, tm, w_tail.shape[1], "input_projection_tail")

    o_sb = _sb_attention(h, batch, seq, sbh, 0)

    ba = h_tail[:, 2 * memw:2 * memw + 2 * dnh].reshape(batch, seq, 2 * dnh)
    ba = ba.transpose(0, 2, 1).reshape(batch, 2 * dnh, seq // DN_CHUNK, DN_CHUNK)
    o_dn = _deltanet(h, conv_w, ba, a_log, dt_bias, dn_norm_w, batch, seq, dnh, 4 * sbh)

    kv = _tiled_matmul(_proj_body, mem.reshape(batch * n_mem, d).astype(BF16), w_mem_kv.astype(BF16), [], 0, BF16,
                       _tile(batch * n_mem, 512), _tile(2 * memw, 1024), "memory_kv_projection")
    o_m = _mem_attention(h_tail, kv, batch, seq, n_mem, memw, _tile(seq, 512))

    n_br = w_gate.shape[0]
    wg = w_gate.transpose(1, 0, 2).reshape(d, n_br * d).astype(BF16)
    tn = _tile(d, 1024)
    gates = _tiled_matmul(_gate_body, xb, wg, [(b_gate.reshape(1, n_br * d), (1, tn), lambda j, i: (0, j))],
                          0, BF16, tm, tn, "branch_gates")
    y = _merge(o_sb, o_dn, o_m, w_up_sb.astype(BF16), w_up_dn.astype(BF16), w_up_mem.astype(BF16), gates, tm,
               _tile(d, 512))
    tn_out = _tile(d, 512)
    r = _tiled_matmul(functools.partial(_resid_body, alpha=alpha), y, w_out.astype(BF16),
                      [(x2, (tm, tn_out), lambda j, i: (i, j))], 0, F32, tm, tn_out, "output_projection")
    return _layernorm(r, ln_g, ln_b, _tile(m, 256)).reshape(batch, seq, d)


def kernel(x, mem, w_in, conv_w, a_log, dt_bias, dn_norm_w, w_mem_kv, w_gate, b_gate, w_up_sb, w_up_dn, w_up_mem,
           w_out, ln_g, ln_b):
    depth = w_in.shape[0]
    alpha = (2.0 * depth) ** 0.25
    for l in range(depth):
        x = _layer(x, mem, w_in[l], conv_w[l], a_log[l], dt_bias[l], dn_norm_w[l], w_mem_kv[l], w_gate[l],
                   b_gate[l], w_up_sb[l], w_up_dn[l], w_up_mem[l], w_out[l], ln_g[l], ln_b[l], alpha)
    return x
```

```python
import functools

import numpy as np
import jax
import jax.numpy as jnp
from jax import lax
from jax.experimental import pallas as pl
from jax.experimental.pallas import tpu as pltpu

HEAD_DIM = 128
CONV_K = 4
CONV_HALO = 8
MEM_HEADS = 4
LN_EPS = 1e-5
RMS_EPS = 1e-6
L2_EPS = 1e-6
LOG2E = 1.4426950408889634
SOFTPLUS2_CLAMP = 64.0
MASKED_LOG2_WEIGHT = -1e30
SB_TRI_EXTRA_ROWS = 16

DN_CHUNK = 128
DN_GROUP = 8
SB_BLOCK = 256
SB_HEADS_PER_STEP = 4
V7X_VMEM_LIMIT_BYTES = 48 * 1024 * 1024

F32 = jnp.float32
BF16 = jnp.bfloat16


def _params(n_axes):
    return pltpu.CompilerParams(dimension_semantics=("arbitrary",) * n_axes,
                                vmem_limit_bytes=V7X_VMEM_LIMIT_BYTES)


def _dot(a, b):
    return jnp.dot(a, b, preferred_element_type=F32)


def _dot_nt(a, b):
    return lax.dot_general(a, b, (((1,), (1,)), ((), ())), preferred_element_type=F32)


def _sigmoid(x):
    return 1.0 / (1.0 + jnp.exp(-x))


def _softplus(x):
    return jnp.maximum(x, 0.0) + jnp.log(1.0 + jnp.exp(-jnp.abs(x)))


def _softplus2(x):
    return jnp.maximum(x, jnp.log(1.0 + jnp.exp2(jnp.minimum(x, SOFTPLUS2_CLAMP))) * LOG2E)


def _tile(n, pref):
    t = min(n, pref)
    assert n % t == 0, (n, t)
    return t


def _tiled_matmul(body, a, b, extra, out_dtype, tm, tn, name):
    m, k = a.shape
    n = b.shape[1]
    assert m % tm == 0 and n % tn == 0
    grid = (n // tn, m // tm)
    in_specs = [pl.BlockSpec((tm, k), lambda j, i: (i, 0)),
                pl.BlockSpec((k, tn), lambda j, i: (0, j))]
    in_specs += [pl.BlockSpec(blk, imap) for _, blk, imap in extra]
    return pl.pallas_call(
        body,
        grid=grid,
        in_specs=in_specs,
        out_specs=pl.BlockSpec((tm, tn), lambda j, i: (i, j)),
        out_shape=jax.ShapeDtypeStruct((m, n), out_dtype),
        compiler_params=_params(2),
        name=name,
    )(a, b, *[e[0] for e in extra])


def _proj_scale_body(a_ref, b_ref, s_ref, o_ref):
    o_ref[...] = (_dot(a_ref[...], b_ref[...]) * s_ref[...]).astype(o_ref.dtype)


def _proj_body(a_ref, b_ref, o_ref):
    o_ref[...] = _dot(a_ref[...], b_ref[...]).astype(o_ref.dtype)


def _gate_body(a_ref, b_ref, bias_ref, o_ref):
    o_ref[...] = _sigmoid(_dot(a_ref[...], b_ref[...]) + bias_ref[...]).astype(o_ref.dtype)


def _resid_body(a_ref, b_ref, x_ref, o_ref, *, alpha):
    o_ref[...] = alpha * x_ref[...] + _dot(a_ref[...], b_ref[...])


def _merge_body(osb_ref, odn_ref, om_ref, wsb_ref, wdn_ref, wm_ref, g0_ref, g1_ref, g2_ref, o_ref):
    y = g0_ref[...].astype(F32) * _dot(osb_ref[...], wsb_ref[...])
    y += g1_ref[...].astype(F32) * _dot(odn_ref[...], wdn_ref[...])
    y += g2_ref[...].astype(F32) * _dot(om_ref[...], wm_ref[...])
    o_ref[...] = y.astype(o_ref.dtype)


def _merge(o_sb, o_dn, o_m, w_sb, w_dn, w_m, gates, tm, tn):
    m = o_sb.shape[0]
    d = w_sb.shape[1]
    nb = d // tn
    row = lambda w: pl.BlockSpec((tm, w), lambda j, i: (i, 0))
    colw = lambda w: pl.BlockSpec((w, tn), lambda j, i: (0, j))
    gate = lambda g: pl.BlockSpec((tm, tn), lambda j, i: (i, g * nb + j))
    return pl.pallas_call(
        _merge_body,
        grid=(nb, m // tm),
        in_specs=[row(o_sb.shape[1]), row(o_dn.shape[1]), row(o_m.shape[1]),
                  colw(w_sb.shape[0]), colw(w_dn.shape[0]), colw(w_m.shape[0]),
                  gate(0), gate(1), gate(2)],
        out_specs=pl.BlockSpec((tm, tn), lambda j, i: (i, j)),
        out_shape=jax.ShapeDtypeStruct((m, d), BF16),
        compiler_params=_params(2),
        name="branch_merge",
    )(o_sb, o_dn, o_m, w_sb, w_dn, w_m, gates, gates, gates)


def _layernorm_body(r_ref, g_ref, b_ref, o_ref):
    r = r_ref[...]
    mu = jnp.mean(r, axis=-1, keepdims=True)
    rc = r - mu
    var = jnp.mean(rc * rc, axis=-1, keepdims=True)
    o_ref[...] = rc * lax.rsqrt(var + LN_EPS) * g_ref[...] + b_ref[...]


def _layernorm(r, g, b, tm):
    m, d = r.shape
    return pl.pallas_call(
        _layernorm_body,
        grid=(m // tm,),
        in_specs=[pl.BlockSpec((tm, d), lambda i: (i, 0)),
                  pl.BlockSpec((1, d), lambda i: (0, 0)),
                  pl.BlockSpec((1, d), lambda i: (0, 0))],
        out_specs=pl.BlockSpec((tm, d), lambda i: (i, 0)),
        out_shape=jax.ShapeDtypeStruct((m, d), F32),
        compiler_params=_params(1),
        name="post_layernorm",
    )(r, g.reshape(1, d), b.reshape(1, d))


def _sb_body(q_ref, k_ref, v_ref, z_ref, tri_ref, o_ref, vt_ref, acc_ref, lsig_ref, sp_ref, *, tk, nq, hps):
    qi = pl.program_id(2)
    heads = range(hps)
    lanes = lambda hh: slice(hh * HEAD_DIM, (hh + 1) * HEAD_DIM)

    @pl.when(qi == 0)
    def _():
        def transpose_block(c, _):
            blk = v_ref[pl.ds(pl.multiple_of(c * tk, tk), tk), :].astype(F32)
            for hh in heads:
                vt_ref[hh, c] = blk[:, lanes(hh)].T.astype(BF16)
            return 0
        lax.fori_loop(0, nq, transpose_block, 0)

    qs = [q_ref[:, lanes(hh)] for hh in heads]
    key_pos = lax.broadcasted_iota(jnp.int32, (tk, tk), 0)
    qry_pos = lax.broadcasted_iota(jnp.int32, (tk, tk), 1)
    before = key_pos < qry_pos

    def score_dots(j):
        kb = k_ref[pl.ds(pl.multiple_of(j * tk, tk), tk), :]
        return [_dot_nt(kb[:, lanes(hh)], qs[hh]) for hh in heads]

    def scores_store(zts, slot, diagonal):
        for hh in heads:
            sp = _softplus2(zts[hh])
            lsig = zts[hh] - sp
            if diagonal:
                sp = jnp.where(before, sp, 0.0)
                lsig = jnp.where(before, lsig, MASKED_LOG2_WEIGHT)
            lsig_ref[slot, hh] = lsig
            sp_ref[slot, hh] = sp.astype(BF16)

    def suffix_dots(slot):
        return [_dot(tri_ref[...], sp_ref[slot, hh]) for hh in heads]

    def finish(j, slot, sufs, carries):
        out = []
        for hh in heads:
            w = jnp.exp2(lsig_ref[slot, hh] - sufs[hh][:tk] - carries[hh])
            acc_ref[hh] += _dot(vt_ref[hh, j], w.astype(BF16))
            out.append(carries[hh] + sufs[hh][tk:tk + 1])
        return tuple(out)

    def step(t, carries):
        slot = t & 1
        sufs = suffix_dots(slot)
        zts = score_dots(qi - t - 1)
        carries = finish(qi - t, slot, sufs, carries)
        scores_store(zts, 1 - slot, False)
        return carries

    acc_ref[...] = jnp.zeros(acc_ref.shape, F32)
    scores_store(score_dots(qi), 0, True)
    carries = lax.fori_loop(0, qi, step, (jnp.zeros((1, tk), F32),) * hps)
    last = qi & 1
    finish(0, last, suffix_dots(last), carries)

    for hh in heads:
        z = z_ref[:, lanes(hh)].astype(F32)
        o_ref[:, lanes(hh)] = (acc_ref[hh].T * (z * _sigmoid(z))).astype(o_ref.dtype)


def _sb_attention(h, batch, seq, heads, col0):
    t = _tile(seq, SB_BLOCK)
    nq = seq // t
    hps = SB_HEADS_PER_STEP if heads % SB_HEADS_PER_STEP == 0 and col0 % SB_HEADS_PER_STEP == 0 else 1
    ng = heads // hps
    c0 = col0 // hps
    w = hps * HEAD_DIM
    tri = np.zeros((t + SB_TRI_EXTRA_ROWS, t), np.float32)
    tri[:t] = np.triu(np.ones((t, t), np.float32), 1)
    tri[t] = 1.0
    tri = jnp.asarray(tri, BF16)
    body = functools.partial(_sb_body, tk=t, nq=nq, hps=hps)
    whole_seq = lambda g: pl.BlockSpec((seq, w), lambda b, h_, i: (b, c0 + g * ng + h_), pipeline_mode=pl.Buffered(1))
    return pl.pallas_call(
        body,
        grid=(batch, ng, nq),
        in_specs=[pl.BlockSpec((t, w), lambda b, h_, i: (b * nq + i, c0 + h_)),
                  whole_seq(1), whole_seq(2),
                  pl.BlockSpec((t, w), lambda b, h_, i: (b * nq + i, c0 + 3 * ng + h_)),
                  pl.BlockSpec((t + SB_TRI_EXTRA_ROWS, t), lambda b, h_, i: (0, 0))],
        out_specs=pl.BlockSpec((t, w), lambda b, h_, i: (b * nq + i, h_)),
        out_shape=jax.ShapeDtypeStruct((batch * seq, heads * HEAD_DIM), BF16),
        scratch_shapes=[pltpu.VMEM((hps, nq, HEAD_DIM, t), BF16), pltpu.VMEM((hps, HEAD_DIM, t), F32),
                        pltpu.VMEM((2, hps, t, t), F32), pltpu.VMEM((2, hps, t, t), BF16)],
        compiler_params=_params(3),
        name="stickbreak_attention",
    )(h, h, h, h, tri)


_M_INCL, _M_STRICT, _M_PAIR, _M_LEVEL0 = 0, 1, 2, 3
_N_LEVELS = int(np.log2(DN_CHUNK)) - 1
_M_EYE = _M_LEVEL0 + _N_LEVELS
_M_UPPER = _M_EYE + 1
_N_MASKS = _M_UPPER + 1


def _dn_masks():
    c = DN_CHUNK
    i = np.arange(c)[:, None]
    j = np.arange(c)[None, :]
    masks = [i >= j, i > j, (i % 2 == 1) & (j == i - 1)]
    b = 4
    while b <= c:
        masks.append((i // b == j // b) & (i % b >= b // 2) & (j % b < b // 2))
        b *= 2
    masks += [i == j, i <= j]
    assert len(masks) == _N_MASKS
    return np.stack(masks).astype(np.float32)


def _split3(x):
    p1 = x.astype(BF16)
    r = x - p1.astype(F32)
    p2 = r.astype(BF16)
    p3 = (r - p2.astype(F32)).astype(BF16)
    return p1, p2, p3


def _dn_body(q_ref, k_ref, v_ref, z_ref, cq_ref, ck_ref, cv_ref, b_ref, a_ref, alog_ref, dtb_ref, nw_ref,
             msk_ref, o_ref, gc_ref, beta_ref, gl_ref, xf_ref, lhs_ref, add_ref, *, seq):
    c = DN_CHUNK
    rows = DN_GROUP * c
    n_groups = seq // rows

    g = -jnp.exp(alog_ref[0]) * _softplus(a_ref[0, 0] + dtb_ref[0])
    upper = msk_ref[_M_UPPER].astype(BF16)
    ones = jnp.ones((c, c), BF16)
    pieces = _split3(g)
    gc_ref[...] = sum(_dot(p, upper) for p in pieces)
    gl_ref[...] = sum(_dot(p, ones) for p in pieces)
    beta_ref[...] = _sigmoid(b_ref[0, 0])

    def conv_silu(idx, x_ref, cw_ref, it, r0):
        if isinstance(it, int):
            assert it == 0
            xf_ref[idx, 0:CONV_HALO, :] = jnp.zeros((CONV_HALO, HEAD_DIM), F32)
        else:
            h0 = pl.multiple_of(r0 - CONV_HALO, CONV_HALO)
            xf_ref[idx, 0:CONV_HALO, :] = x_ref[pl.ds(h0, CONV_HALO), :].astype(F32)
            r0 = pl.multiple_of(r0, rows)
        xf_ref[idx, CONV_HALO:, :] = x_ref[pl.ds(r0, rows), :].astype(F32)
        cw = cw_ref[...]
        y = None
        for s in range(CONV_K):
            tap = xf_ref[idx, CONV_HALO - s:CONV_HALO - s + rows, :] * cw[CONV_K - 1 - s:CONV_K - s, :]
            y = tap if y is None else y + tap
        return y * _sigmoid(y)

    def l2norm(t):
        return t * lax.rsqrt(jnp.sum(t * t, axis=-1, keepdims=True) + L2_EPS)

    def local_work(it, slot, hooks):
        hooks = list(hooks)
        stage = [0]

        def stage_done():
            stage[0] += 1
            if hooks and stage[0] % 2 == 0:
                hooks.pop(0)()

        r0 = it * rows
        qg = l2norm(conv_silu(0, q_ref, cq_ref, it, r0)) * (HEAD_DIM ** -0.5)
        kg = l2norm(conv_silu(1, k_ref, ck_ref, it, r0))
        vg = conv_silu(2, v_ref, cv_ref, it, r0)
        cs = range(DN_GROUP)
        sl = lambda ci: slice(ci * c, (ci + 1) * c)
        qs, ks, vs = [qg[sl(ci)] for ci in cs], [kg[sl(ci)] for ci in cs], [vg[sl(ci)] for ci in cs]
        gls, gc_cols, beta_cols, decays = [], [], [], []
        for ci in cs:
            chunk = it * DN_GROUP + ci
            gc_row = jnp.broadcast_to(gc_ref[pl.ds(chunk, 1), :], (c, c))
            gc_col = gc_row.T
            incl = msk_ref[_M_INCL]
            gls.append(gl_ref[pl.ds(chunk, 1), :])
            gc_cols.append(gc_col)
            beta_cols.append(jnp.broadcast_to(beta_ref[pl.ds(chunk, 1), :], (c, c)).T)
            decays.append(jnp.exp((gc_col - gc_row) * incl) * incl)
        kbs = [ks[ci] * beta_cols[ci] for ci in cs]
        prods = [_dot_nt(jnp.concatenate([qs[ci], kbs[ci]], axis=0).astype(BF16), ks[ci].astype(BF16)) for ci in cs]
        stage_done()
        a_ins = [prods[ci][:c] * decays[ci] for ci in cs]
        a_los = [prods[ci][c:] * decays[ci] * msk_ref[_M_STRICT] for ci in cs]
        t_invs = [msk_ref[_M_EYE] - a_lo * msk_ref[_M_PAIR] for a_lo in a_los]
        for lvl in range(_N_LEVELS):
            offs = [(a_lo * msk_ref[_M_LEVEL0 + lvl]).astype(BF16) for a_lo in a_los]
            tbs = [t_inv.astype(BF16) for t_inv in t_invs]
            xs = [_dot(tbs[ci], offs[ci]).astype(BF16) for ci in cs]
            stage_done()
            t_invs = [t_invs[ci] - _dot(xs[ci], tbs[ci]) for ci in cs]
            stage_done()
        eg_cols = [jnp.exp(gc_col) for gc_col in gc_cols]
        rhss = [jnp.concatenate([kbs[ci] * eg_cols[ci], vs[ci] * beta_cols[ci]], axis=1).astype(BF16) for ci in cs]
        wus = [_dot(t_invs[ci].astype(BF16), rhss[ci]).astype(BF16) for ci in cs]
        stage_done()
        k_ends = [ks[ci] * jnp.exp(gls[ci] - gc_cols[ci]) for ci in cs]
        pns = [_dot(k_ends[ci].T.astype(BF16), wus[ci]) for ci in cs]
        stage_done()
        aw_aus = [_dot(a_ins[ci].astype(BF16), wus[ci]) for ci in cs]
        stage_done()
        for ci in cs:
            q_eff = qs[ci] * eg_cols[ci] - aw_aus[ci][:, :HEAD_DIM]
            lhs_ref[slot, ci] = jnp.concatenate([q_eff, pns[ci][:, :HEAD_DIM]], axis=0).astype(BF16)
            add_ref[slot, ci] = jnp.concatenate([aw_aus[ci][:, HEAD_DIM:], pns[ci][:, HEAD_DIM:]], axis=0)
        while hooks:
            hooks.pop(0)()

    def scan_chunk(it, slot, ci, state):
        both = _dot(lhs_ref[slot, ci], state.astype(BF16))
        o = both[:c] + add_ref[slot, ci, :c]
        gl = gl_ref[pl.ds(it * DN_GROUP + ci, 1), :]
        state = state * jnp.exp(gl) - both[c:] + add_ref[slot, ci, c:]
        o = o * lax.rsqrt(jnp.mean(o * o, axis=-1, keepdims=True) + RMS_EPS) * nw_ref[...]
        r0 = it * rows + ci * c
        if not isinstance(r0, int):
            r0 = pl.multiple_of(r0, c)
        zr = z_ref[pl.ds(r0, c), :].astype(F32)
        o_ref[pl.ds(r0, c), :] = (o * (zr * _sigmoid(zr))).astype(o_ref.dtype)
        return state

    def step(it, state):
        box = [state]

        def scan_hook(ci):
            box[0] = scan_chunk(it - 1, 1 - (it & 1), ci, box[0])

        local_work(it, it & 1, [functools.partial(scan_hook, ci) for ci in range(DN_GROUP)])
        return box[0]

    local_work(0, 0, [])
    state = lax.fori_loop(1, n_groups, step, jnp.zeros((HEAD_DIM, HEAD_DIM), F32))
    for ci in range(DN_GROUP):
        state = scan_chunk(n_groups - 1, (n_groups - 1) & 1, ci, state)


def _deltanet(h, conv_w, ba, a_log, dt_bias, norm_w, batch, seq, heads, col0):
    c = DN_CHUNK
    assert seq % (DN_GROUP * c) == 0
    n_chunks = seq // c
    masks = jnp.asarray(_dn_masks())
    lane = lambda v: jnp.broadcast_to(v.astype(F32)[:, None, None], (heads, 1, c))
    body = functools.partial(_dn_body, seq=seq)
    hcol = lambda g: pl.BlockSpec((seq, HEAD_DIM), lambda b, h_: (b, col0 + g * heads + h_))
    cw = lambda g: pl.BlockSpec((CONV_K, HEAD_DIM), lambda b, h_: (0, g * heads + h_))
    return pl.pallas_call(
        body,
        grid=(batch, heads),
        in_specs=[hcol(0), hcol(1), hcol(2), hcol(3), cw(0), cw(1), cw(2),
                  pl.BlockSpec((1, 1, n_chunks, c), lambda b, h_: (b, h_, 0, 0)),
                  pl.BlockSpec((1, 1, n_chunks, c), lambda b, h_: (b, heads + h_, 0, 0)),
                  pl.BlockSpec((1, 1, c), lambda b, h_: (h_, 0, 0)),
                  pl.BlockSpec((1, 1, c), lambda b, h_: (h_, 0, 0)),
                  pl.BlockSpec((1, HEAD_DIM), lambda b, h_: (0, 0)),
                  pl.BlockSpec((_N_MASKS, c, c), lambda b, h_: (0, 0, 0))],
        out_specs=pl.BlockSpec((seq, HEAD_DIM), lambda b, h_: (b, h_)),
        out_shape=jax.ShapeDtypeStruct((batch * seq, heads * HEAD_DIM), BF16),
        scratch_shapes=[pltpu.VMEM((n_chunks, c), F32)] * 3
        + [pltpu.VMEM((3, CONV_HALO + DN_GROUP * c, HEAD_DIM), F32),
           pltpu.VMEM((2, DN_GROUP, 2 * c, HEAD_DIM), BF16), pltpu.VMEM((2, DN_GROUP, 2 * c, HEAD_DIM), F32)],
        compiler_params=_params(2),
        name="gated_deltanet",
    )(h, h, h, h, conv_w, conv_w, conv_w, ba, ba, lane(a_log), lane(dt_bias), norm_w.reshape(1, HEAD_DIM), masks)


def _mem_body(q_ref, z_ref, kv_ref, o_ref, *, width):
    for hh in range(width // HEAD_DIM):
        sl = slice(hh * HEAD_DIM, (hh + 1) * HEAD_DIM)
        q = (q_ref[:, sl] * (HEAD_DIM ** -0.5)).astype(BF16)
        k = kv_ref[:, sl]
        v = kv_ref[:, width + hh * HEAD_DIM:width + (hh + 1) * HEAD_DIM]
        s = _dot_nt(q, k)
        p = jnp.exp(s - jnp.max(s, axis=-1, keepdims=True))
        o = _dot(p.astype(BF16), v) * (1.0 / jnp.sum(p, axis=-1, keepdims=True))
        z = z_ref[:, sl]
        o_ref[:, sl] = (o * (z * _sigmoid(z))).astype(o_ref.dtype)


def _mem_attention(h_tail, kv, batch, seq, n_mem, width, tq):
    nq = seq // tq
    body = functools.partial(_mem_body, width=width)
    return pl.pallas_call(
        body,
        grid=(batch, nq),
        in_specs=[pl.BlockSpec((tq, width), lambda b, i: (b * nq + i, 0)),
                  pl.BlockSpec((tq, width), lambda b, i: (b * nq + i, 1)),
                  pl.BlockSpec((n_mem, 2 * width), lambda b, i: (b, 0))],
        out_specs=pl.BlockSpec((tq, width), lambda b, i: (b * nq + i, 0)),
        out_shape=jax.ShapeDtypeStruct((batch * seq, width), BF16),
        compiler_params=_params(2),
        name="memory_attention",
    )(h_tail, h_tail, kv)


def _layer(x, mem, w_in, conv_w, a_log, dt_bias, dn_norm_w, w_mem_kv, w_gate, b_gate,
           w_up_sb, w_up_dn, w_up_mem, w_out, ln_g, ln_b, alpha):
    batch, seq, d = x.shape
    n_mem = mem.shape[1]
    m = batch * seq
    sbw, dnw, memw = w_up_sb.shape[0], w_up_dn.shape[0], w_up_mem.shape[0]
    sbh, dnh = sbw // HEAD_DIM, dnw // HEAD_DIM
    n_main = 4 * sbw + 4 * dnw
    ba0 = n_main
    qm0 = ba0 + 2 * dnh
    assert w_in.shape[1] == qm0 + 2 * memw

    x2 = x.reshape(m, d)
    xb = x2.astype(BF16)
    tm = _tile(m, 1024)

    w_main = w_in[:, :n_main].astype(BF16)
    col_scale = jnp.concatenate([jnp.full((sbw,), HEAD_DIM ** -0.5 * LOG2E, F32), jnp.ones((n_main - sbw,), F32)])
    h = _tiled_matmul(_proj_scale_body, xb, w_main,
                      [(col_scale.reshape(1, n_main), (1, _tile(n_main, 1024)), lambda j, i: (0, j))],
                      BF16, tm, _tile(n_main, 1024), "input_projection")
    pad = (-2 * dnh) % HEAD_DIM
    w_tail = jnp.concatenate([w_in[:, qm0:], w_in[:, ba0:qm0], jnp.zeros((d, pad), F32)], axis=1).astype(BF16)
    h_tail = _tiled_matmul(_proj_body, xb, w_tail, [], F32, tm, w_tail.shape[1], "input_projection_tail")

    o_sb = _sb_attention(h, batch, seq, sbh, 0)

    ba = h_tail[:, 2 * memw:2 * memw + 2 * dnh].reshape(batch, seq, 2 * dnh)
    ba = ba.transpose(0, 2, 1).reshape(batch, 2 * dnh, seq // DN_CHUNK, DN_CHUNK)
    o_dn = _deltanet(h, conv_w, ba, a_log, dt_bias, dn_norm_w, batch, seq, dnh, 4 * sbh)

    kv = _tiled_matmul(_proj_body, mem.reshape(batch * n_mem, d).astype(BF16), w_mem_kv.astype(BF16), [], BF16,
                       _tile(batch * n_mem, 512), _tile(2 * memw, 1024), "memory_kv_projection")
    o_m = _mem_attention(h_tail, kv, batch, seq, n_mem, memw, _tile(seq, 512))

    n_br = w_gate.shape[0]
    wg = w_gate.transpose(1, 0, 2).reshape(d, n_br * d).astype(BF16)
    tn = _tile(d, 1024)
    gates = _tiled_matmul(_gate_body, xb, wg, [(b_gate.reshape(1, n_br * d), (1, tn), lambda j, i: (0, j))],
                          BF16, tm, tn, "branch_gates")
    y = _merge(o_sb, o_dn, o_m, w_up_sb.astype(BF16), w_up_dn.astype(BF16), w_up_mem.astype(BF16), gates, tm,
               _tile(d, 512))
    tn_out = _tile(d, 512)
    r = _tiled_matmul(functools.partial(_resid_body, alpha=alpha), y, w_out.astype(BF16),
                      [(x2, (tm, tn_out), lambda j, i: (i, j))], F32, tm, tn_out, "output_projection")
    return _layernorm(r, ln_g, ln_b, _tile(m, 256)).reshape(batch, seq, d)


def kernel(x, mem, w_in, conv_w, a_log, dt_bias, dn_norm_w, w_mem_kv, w_gate, b_gate, w_up_sb, w_up_dn, w_up_mem,
           w_out, ln_g, ln_b):
    depth = w_in.shape[0]
    alpha = (2.0 * depth) ** 0.25
    for l in range(depth):
        x = _layer(x, mem, w_in[l], conv_w[l], a_log[l], dt_bias[l], dn_norm_w[l], w_mem_kv[l], w_gate[l],
                   b_gate[l], w_up_sb[l], w_up_dn[l], w_up_mem[l], w_out[l], ln_g[l], ln_b[l], alpha)
    return x
```

```python
import functools

import numpy as np
import jax
import jax.numpy as jnp
from jax import lax
from jax.experimental import pallas as pl
from jax.experimental.pallas import tpu as pltpu

HEAD_DIM = 128
CONV_K = 4
CONV_HALO = 8
MEM_HEADS = 4
LN_EPS = 1e-5
RMS_EPS = 1e-6
L2_EPS = 1e-6
LOG2E = 1.4426950408889634
SOFTPLUS2_CLAMP = 64.0
MASKED_LOG2_WEIGHT = -1e30
SB_TRI_EXTRA_ROWS = 16
SB_UNDERFLOW_LOG2 = 1100.0

DN_CHUNK = 128
DN_GROUP = 8
SB_BLOCK = 256
SB_HEADS_PER_STEP = 4
V7X_VMEM_LIMIT_BYTES = 48 * 1024 * 1024

F32 = jnp.float32
BF16 = jnp.bfloat16


def _params(n_axes):
    return pltpu.CompilerParams(dimension_semantics=("arbitrary",) * n_axes,
                                vmem_limit_bytes=V7X_VMEM_LIMIT_BYTES)


def _dot(a, b):
    return jnp.dot(a, b, preferred_element_type=F32)


def _dot_nt(a, b):
    return lax.dot_general(a, b, (((1,), (1,)), ((), ())), preferred_element_type=F32)


def _sigmoid(x):
    return 1.0 / (1.0 + jnp.exp(-x))


def _softplus(x):
    return jnp.maximum(x, 0.0) + jnp.log(1.0 + jnp.exp(-jnp.abs(x)))


def _softplus2(x):
    return jnp.maximum(x, jnp.log(1.0 + jnp.exp2(jnp.minimum(x, SOFTPLUS2_CLAMP))) * LOG2E)


def _tile(n, pref):
    t = min(n, pref)
    assert n % t == 0, (n, t)
    return t


def _tiled_matmul(body, a, b, extra, out_dtype, tm, tn, name, n_cols=None):
    m, k = a.shape
    if b.ndim == 3:
        nb = b.shape[2] // tn
        n = b.shape[0] * b.shape[2]
        b_spec = pl.BlockSpec((None, k, tn), lambda j, i: (j // nb, 0, j % nb))
    else:
        n = b.shape[1] if n_cols is None else n_cols
        b_spec = pl.BlockSpec((k, tn), lambda j, i: (0, j))
    assert m % tm == 0 and n % tn == 0
    grid = (n // tn, m // tm)
    in_specs = [pl.BlockSpec((tm, k), lambda j, i: (i, 0)), b_spec]
    in_specs += [pl.BlockSpec(blk, imap) for _, blk, imap in extra]
    return pl.pallas_call(
        body,
        grid=grid,
        in_specs=in_specs,
        out_specs=pl.BlockSpec((tm, tn), lambda j, i: (i, j)),
        out_shape=jax.ShapeDtypeStruct((m, n), out_dtype),
        compiler_params=_params(2),
        name=name,
    )(a, b, *[e[0] for e in extra])


def _proj_scale_body(a_ref, b_ref, s_ref, o_ref):
    o_ref[...] = (_dot(a_ref[...], b_ref[...]) * s_ref[...]).astype(o_ref.dtype)


def _proj_body(a_ref, b_ref, o_ref):
    o_ref[...] = _dot(a_ref[...], b_ref[...]).astype(o_ref.dtype)


def _gate_body(a_ref, b_ref, bias_ref, o_ref):
    o_ref[...] = _sigmoid(_dot(a_ref[...], b_ref[...]) + bias_ref[...]).astype(o_ref.dtype)


def _resid_body(a_ref, b_ref, x_ref, o_ref, *, alpha):
    o_ref[...] = alpha * x_ref[...] + _dot(a_ref[...], b_ref[...])


def _merge_body(osb_ref, odn_ref, om_ref, wsb_ref, wdn_ref, wm_ref, g0_ref, g1_ref, g2_ref, o_ref):
    y = g0_ref[...].astype(F32) * _dot(osb_ref[...], wsb_ref[...])
    y += g1_ref[...].astype(F32) * _dot(odn_ref[...], wdn_ref[...])
    y += g2_ref[...].astype(F32) * _dot(om_ref[...], wm_ref[...])
    o_ref[...] = y.astype(o_ref.dtype)


def _merge(o_sb, o_dn, o_m, w_sb, w_dn, w_m, gates, tm, tn):
    m = o_sb.shape[0]
    d = w_sb.shape[1]
    nb = d // tn
    row = lambda w: pl.BlockSpec((tm, w), lambda j, i: (i, 0))
    colw = lambda w: pl.BlockSpec((w, tn), lambda j, i: (0, j))
    gate = lambda g: pl.BlockSpec((tm, tn), lambda j, i: (i, g * nb + j))
    return pl.pallas_call(
        _merge_body,
        grid=(nb, m // tm),
        in_specs=[row(o_sb.shape[1]), row(o_dn.shape[1]), row(o_m.shape[1]),
                  colw(w_sb.shape[0]), colw(w_dn.shape[0]), colw(w_m.shape[0]),
                  gate(0), gate(1), gate(2)],
        out_specs=pl.BlockSpec((tm, tn), lambda j, i: (i, j)),
        out_shape=jax.ShapeDtypeStruct((m, d), BF16),
        compiler_params=_params(2),
        name="branch_merge",
    )(o_sb, o_dn, o_m, w_sb, w_dn, w_m, gates, gates, gates)


def _layernorm_body(r_ref, g_ref, b_ref, o_ref):
    r = r_ref[...]
    mu = jnp.mean(r, axis=-1, keepdims=True)
    rc = r - mu
    var = jnp.mean(rc * rc, axis=-1, keepdims=True)
    o_ref[...] = rc * lax.rsqrt(var + LN_EPS) * g_ref[...] + b_ref[...]


def _layernorm(r, g, b, tm):
    m, d = r.shape
    return pl.pallas_call(
        _layernorm_body,
        grid=(m // tm,),
        in_specs=[pl.BlockSpec((tm, d), lambda i: (i, 0)),
                  pl.BlockSpec((1, d), lambda i: (0, 0)),
                  pl.BlockSpec((1, d), lambda i: (0, 0))],
        out_specs=pl.BlockSpec((tm, d), lambda i: (i, 0)),
        out_shape=jax.ShapeDtypeStruct((m, d), F32),
        compiler_params=_params(1),
        name="post_layernorm",
    )(r, g.reshape(1, d), b.reshape(1, d))


def _sb_body(q_ref, k_ref, v_ref, z_ref, tri_ref, o_ref, vt_ref, acc_ref, lsig_ref, sp_ref, *, tk, nq, hps):
    qi = pl.program_id(2)
    heads = range(hps)
    lanes = lambda hh: slice(hh * HEAD_DIM, (hh + 1) * HEAD_DIM)

    @pl.when(qi == 0)
    def _():
        def transpose_block(c, _):
            blk = v_ref[pl.ds(pl.multiple_of(c * tk, tk), tk), :].astype(F32)
            for hh in heads:
                vt_ref[hh, c] = blk[:, lanes(hh)].T.astype(BF16)
            return 0
        lax.fori_loop(0, nq, transpose_block, 0)

    qs = [q_ref[:, lanes(hh)] for hh in heads]
    key_pos = lax.broadcasted_iota(jnp.int32, (tk, tk), 0)
    qry_pos = lax.broadcasted_iota(jnp.int32, (tk, tk), 1)
    before = key_pos < qry_pos

    def score_dots(j):
        kb = k_ref[pl.ds(pl.multiple_of(j * tk, tk), tk), :]
        return [_dot_nt(kb[:, lanes(hh)], qs[hh]) for hh in heads]

    def scores_store(zts, slot, diagonal):
        for hh in heads:
            sp = _softplus2(zts[hh])
            lsig = zts[hh] - sp
            if diagonal:
                sp = jnp.where(before, sp, 0.0)
                lsig = jnp.where(before, lsig, MASKED_LOG2_WEIGHT)
            lsig_ref[slot, hh] = lsig
            sp_ref[slot, hh] = sp.astype(BF16)

    def suffix_dots(slot):
        return [_dot(tri_ref[...], sp_ref[slot, hh]) for hh in heads]

    def finish(j, slot, sufs, carries):
        out = []
        for hh in heads:
            w = jnp.exp2(lsig_ref[slot, hh] - sufs[hh][:tk] - carries[hh])
            acc_ref[hh] += _dot(vt_ref[hh, j], w.astype(BF16))
            out.append(carries[hh] + sufs[hh][tk:tk + 1])
        return tuple(out)

    def step(state):
        t, carries, _ = state
        slot = t & 1
        sufs = suffix_dots(slot)
        zts = score_dots(qi - t - 1)
        carries = finish(qi - t, slot, sufs, carries)
        scores_store(zts, 1 - slot, False)
        smallest = functools.reduce(jnp.minimum, carries)
        return t + 1, carries, jnp.min(smallest) < SB_UNDERFLOW_LOG2

    acc_ref[...] = jnp.zeros(acc_ref.shape, F32)
    scores_store(score_dots(qi), 0, True)
    t, carries, _ = lax.while_loop(lambda s: jnp.logical_and(s[0] < qi, s[2]), step,
                                   (jnp.int32(0), (jnp.zeros((1, tk), F32),) * hps, jnp.bool_(True)))
    last = t & 1
    finish(qi - t, last, suffix_dots(last), carries)

    for hh in heads:
        z = z_ref[:, lanes(hh)].astype(F32)
        o_ref[:, lanes(hh)] = (acc_ref[hh].T * (z * _sigmoid(z))).astype(o_ref.dtype)


def _sb_attention(h, batch, seq, heads, col0):
    t = _tile(seq, SB_BLOCK)
    nq = seq // t
    hps = SB_HEADS_PER_STEP if heads % SB_HEADS_PER_STEP == 0 and col0 % SB_HEADS_PER_STEP == 0 else 1
    ng = heads // hps
    c0 = col0 // hps
    w = hps * HEAD_DIM
    tri = np.zeros((t + SB_TRI_EXTRA_ROWS, t), np.float32)
    tri[:t] = np.triu(np.ones((t, t), np.float32), 1)
    tri[t] = 1.0
    tri = jnp.asarray(tri, BF16)
    body = functools.partial(_sb_body, tk=t, nq=nq, hps=hps)
    whole_seq = lambda g: pl.BlockSpec((seq, w), lambda b, h_, i: (b, c0 + g * ng + h_), pipeline_mode=pl.Buffered(1))
    return pl.pallas_call(
        body,
        grid=(batch, ng, nq),
        in_specs=[pl.BlockSpec((t, w), lambda b, h_, i: (b * nq + i, c0 + h_)),
                  whole_seq(1), whole_seq(2),
                  pl.BlockSpec((t, w), lambda b, h_, i: (b * nq + i, c0 + 3 * ng + h_)),
                  pl.BlockSpec((t + SB_TRI_EXTRA_ROWS, t), lambda b, h_, i: (0, 0))],
        out_specs=pl.BlockSpec((t, w), lambda b, h_, i: (b * nq + i, h_)),
        out_shape=jax.ShapeDtypeStruct((batch * seq, heads * HEAD_DIM), BF16),
        scratch_shapes=[pltpu.VMEM((hps, nq, HEAD_DIM, t), BF16), pltpu.VMEM((hps, HEAD_DIM, t), F32),
                        pltpu.VMEM((2, hps, t, t), F32), pltpu.VMEM((2, hps, t, t), BF16)],
        compiler_params=_params(3),
        name="stickbreak_attention",
    )(h, h, h, h, tri)


_M_INCL, _M_STRICT, _M_PAIR, _M_LEVEL0 = 0, 1, 2, 3
_N_LEVELS = int(np.log2(DN_CHUNK)) - 1
_M_EYE = _M_LEVEL0 + _N_LEVELS
_M_UPPER = _M_EYE + 1
_N_MASKS = _M_UPPER + 1


def _dn_masks():
    c = DN_CHUNK
    i = np.arange(c)[:, None]
    j = np.arange(c)[None, :]
    masks = [i >= j, i > j, (i % 2 == 1) & (j == i - 1)]
    b = 4
    while b <= c:
        masks.append((i // b == j // b) & (i % b >= b // 2) & (j % b < b // 2))
        b *= 2
    masks += [i == j, i <= j]
    assert len(masks) == _N_MASKS
    return np.stack(masks).astype(np.float32)


def _split3(x):
    p1 = x.astype(BF16)
    r = x - p1.astype(F32)
    p2 = r.astype(BF16)
    p3 = (r - p2.astype(F32)).astype(BF16)
    return p1, p2, p3


def _dn_body(q_ref, k_ref, v_ref, z_ref, cq_ref, ck_ref, cv_ref, b_ref, a_ref, alog_ref, dtb_ref, nw_ref,
             msk_ref, o_ref, gc_ref, beta_ref, gl_ref, xf_ref, lhs_ref, add_ref, *, seq):
    c = DN_CHUNK
    rows = DN_GROUP * c
    n_groups = seq // rows

    g = -jnp.exp(alog_ref[0]) * _softplus(a_ref[0, 0] + dtb_ref[0])
    upper = msk_ref[_M_UPPER].astype(BF16)
    ones = jnp.ones((c, c), BF16)
    pieces = _split3(g)
    gc_ref[...] = sum(_dot(p, upper) for p in pieces)
    gl_ref[...] = sum(_dot(p, ones) for p in pieces)
    beta_ref[...] = _sigmoid(b_ref[0, 0])

    def conv_silu(idx, x_ref, cw_ref, r0, first):
        if first:
            xf_ref[idx, 0:CONV_HALO, :] = jnp.zeros((CONV_HALO, HEAD_DIM), F32)
        else:
            h0 = pl.multiple_of(r0 - CONV_HALO, CONV_HALO)
            xf_ref[idx, 0:CONV_HALO, :] = x_ref[pl.ds(h0, CONV_HALO), :].astype(F32)
        xf_ref[idx, CONV_HALO:, :] = x_ref[pl.ds(r0, rows), :].astype(F32)
        cw = cw_ref[...]
        y = None
        for s in range(CONV_K):
            tap = xf_ref[idx, CONV_HALO - s:CONV_HALO - s + rows, :] * cw[CONV_K - 1 - s:CONV_K - s, :]
            y = tap if y is None else y + tap
        return y * _sigmoid(y)

    def l2norm(t):
        return t * lax.rsqrt(jnp.sum(t * t, axis=-1, keepdims=True) + L2_EPS)

    def local_work(it, slot, hooks, first=False):
        hooks = list(hooks)
        stage = [0]

        def stage_done():
            stage[0] += 1
            if hooks and stage[0] % 2 == 0:
                hooks.pop(0)()

        r0 = pl.multiple_of(it * rows, rows)
        qg = l2norm(conv_silu(0, q_ref, cq_ref, r0, first)) * (HEAD_DIM ** -0.5)
        kg = l2norm(conv_silu(1, k_ref, ck_ref, r0, first))
        vg = conv_silu(2, v_ref, cv_ref, r0, first)
        cs = range(DN_GROUP)
        sl = lambda ci: slice(ci * c, (ci + 1) * c)
        qs, ks, vs = [qg[sl(ci)] for ci in cs], [kg[sl(ci)] for ci in cs], [vg[sl(ci)] for ci in cs]
        gls, gc_cols, beta_cols, decays = [], [], [], []
        for ci in cs:
            chunk = it * DN_GROUP + ci
            gc_row = jnp.broadcast_to(gc_ref[pl.ds(chunk, 1), :], (c, c))
            gc_col = gc_row.T
            incl = msk_ref[_M_INCL]
            gls.append(gl_ref[pl.ds(chunk, 1), :])
            gc_cols.append(gc_col)
            beta_cols.append(jnp.broadcast_to(beta_ref[pl.ds(chunk, 1), :], (c, c)).T)
            decays.append(jnp.exp((gc_col - gc_row) * incl) * incl)
        kbs = [ks[ci] * beta_cols[ci] for ci in cs]
        prods = [_dot_nt(jnp.concatenate([qs[ci], kbs[ci]], axis=0).astype(BF16), ks[ci].astype(BF16)) for ci in cs]
        stage_done()
        a_ins = [prods[ci][:c] * decays[ci] for ci in cs]
        a_los = [prods[ci][c:] * decays[ci] * msk_ref[_M_STRICT] for ci in cs]
        t_invs = [msk_ref[_M_EYE] - a_lo * msk_ref[_M_PAIR] for a_lo in a_los]
        for lvl in range(_N_LEVELS):
            offs = [(a_lo * msk_ref[_M_LEVEL0 + lvl]).astype(BF16) for a_lo in a_los]
            tbs = [t_inv.astype(BF16) for t_inv in t_invs]
            xs = [_dot(tbs[ci], offs[ci]).astype(BF16) for ci in cs]
            stage_done()
            t_invs = [t_invs[ci] - _dot(xs[ci], tbs[ci]) for ci in cs]
            stage_done()
        eg_cols = [jnp.exp(gc_col) for gc_col in gc_cols]
        rhss = [jnp.concatenate([kbs[ci] * eg_cols[ci], vs[ci] * beta_cols[ci]], axis=1).astype(BF16) for ci in cs]
        wus = [_dot(t_invs[ci].astype(BF16), rhss[ci]).astype(BF16) for ci in cs]
        stage_done()
        k_ends = [ks[ci] * jnp.exp(gls[ci] - gc_cols[ci]) for ci in cs]
        pns = [_dot(k_ends[ci].T.astype(BF16), wus[ci]) for ci in cs]
        stage_done()
        aw_aus = [_dot(a_ins[ci].astype(BF16), wus[ci]) for ci in cs]
        stage_done()
        for ci in cs:
            q_eff = qs[ci] * eg_cols[ci] - aw_aus[ci][:, :HEAD_DIM]
            lhs_ref[slot, ci] = jnp.concatenate([q_eff, pns[ci][:, :HEAD_DIM]], axis=0).astype(BF16)
            add_ref[slot, ci] = jnp.concatenate([aw_aus[ci][:, HEAD_DIM:], pns[ci][:, HEAD_DIM:]], axis=0)
        while hooks:
            hooks.pop(0)()

    def scan_chunk(it, slot, ci, state):
        both = _dot(lhs_ref[slot, ci], state.astype(BF16))
        o = both[:c] + add_ref[slot, ci, :c]
        gl = gl_ref[pl.ds(it * DN_GROUP + ci, 1), :]
        state = state * jnp.exp(gl) - both[c:] + add_ref[slot, ci, c:]
        o = o * lax.rsqrt(jnp.mean(o * o, axis=-1, keepdims=True) + RMS_EPS) * nw_ref[...]
        r0 = pl.multiple_of(it * rows + ci * c, c)
        zr = z_ref[pl.ds(r0, c), :].astype(F32)
        o_ref[pl.ds(r0, c), :] = (o * (zr * _sigmoid(zr))).astype(o_ref.dtype)
        return state

    def step(it, state):
        box = [state]

        def scan_hook(ci):
            box[0] = scan_chunk(it - 1, 1 - (it & 1), ci, box[0])

        local_work(it, it & 1, [functools.partial(scan_hook, ci) for ci in range(DN_GROUP)])
        return box[0]

    local_work(jnp.int32(0), 0, [], first=True)
    state = lax.fori_loop(1, n_groups, step, jnp.zeros((HEAD_DIM, HEAD_DIM), F32))
    for ci in range(DN_GROUP):
        state = scan_chunk(jnp.int32(n_groups - 1), (n_groups - 1) & 1, ci, state)


def _deltanet(h, conv_w, ba, a_log, dt_bias, norm_w, batch, seq, heads, col0):
    c = DN_CHUNK
    assert seq % (DN_GROUP * c) == 0
    n_chunks = seq // c
    masks = jnp.asarray(_dn_masks())
    lane = lambda v: jnp.broadcast_to(v.astype(F32)[:, None, None], (heads, 1, c))
    body = functools.partial(_dn_body, seq=seq)
    hcol = lambda g: pl.BlockSpec((seq, HEAD_DIM), lambda b, h_: (b, col0 + g * heads + h_))
    cw = lambda g: pl.BlockSpec((CONV_K, HEAD_DIM), lambda b, h_: (0, g * heads + h_))
    return pl.pallas_call(
        body,
        grid=(batch, heads),
        in_specs=[hcol(0), hcol(1), hcol(2), hcol(3), cw(0), cw(1), cw(2),
                  pl.BlockSpec((1, 1, n_chunks, c), lambda b, h_: (b, h_, 0, 0)),
                  pl.BlockSpec((1, 1, n_chunks, c), lambda b, h_: (b, heads + h_, 0, 0)),
                  pl.BlockSpec((1, 1, c), lambda b, h_: (h_, 0, 0)),
                  pl.BlockSpec((1, 1, c), lambda b, h_: (h_, 0, 0)),
                  pl.BlockSpec((1, HEAD_DIM), lambda b, h_: (0, 0)),
                  pl.BlockSpec((_N_MASKS, c, c), lambda b, h_: (0, 0, 0))],
        out_specs=pl.BlockSpec((seq, HEAD_DIM), lambda b, h_: (b, h_)),
        out_shape=jax.ShapeDtypeStruct((batch * seq, heads * HEAD_DIM), BF16),
        scratch_shapes=[pltpu.VMEM((n_chunks, c), F32)] * 3
        + [pltpu.VMEM((3, CONV_HALO + DN_GROUP * c, HEAD_DIM), F32),
           pltpu.VMEM((2, DN_GROUP, 2 * c, HEAD_DIM), BF16), pltpu.VMEM((2, DN_GROUP, 2 * c, HEAD_DIM), F32)],
        compiler_params=_params(2),
        name="gated_deltanet",
    )(h, h, h, h, conv_w, conv_w, conv_w, ba, ba, lane(a_log), lane(dt_bias), norm_w.reshape(1, HEAD_DIM), masks)


def _mem_body(q_ref, z_ref, kv_ref, o_ref, *, width):
    for hh in range(width // HEAD_DIM):
        sl = slice(hh * HEAD_DIM, (hh + 1) * HEAD_DIM)
        q = (q_ref[:, sl] * (HEAD_DIM ** -0.5)).astype(BF16)
        k = kv_ref[:, sl]
        v = kv_ref[:, width + hh * HEAD_DIM:width + (hh + 1) * HEAD_DIM]
        s = _dot_nt(q, k)
        p = jnp.exp(s - jnp.max(s, axis=-1, keepdims=True))
        o = _dot(p.astype(BF16), v) * (1.0 / jnp.sum(p, axis=-1, keepdims=True))
        z = z_ref[:, sl]
        o_ref[:, sl] = (o * (z * _sigmoid(z))).astype(o_ref.dtype)


def _mem_attention(h_tail, kv, batch, seq, n_mem, width, tq):
    nq = seq // tq
    body = functools.partial(_mem_body, width=width)
    return pl.pallas_call(
        body,
        grid=(batch, nq),
        in_specs=[pl.BlockSpec((tq, width), lambda b, i: (b * nq + i, 0)),
                  pl.BlockSpec((tq, width), lambda b, i: (b * nq + i, 1)),
                  pl.BlockSpec((n_mem, 2 * width), lambda b, i: (b, 0))],
        out_specs=pl.BlockSpec((tq, width), lambda b, i: (b * nq + i, 0)),
        out_shape=jax.ShapeDtypeStruct((batch * seq, width), BF16),
        compiler_params=_params(2),
        name="memory_attention",
    )(h_tail, h_tail, kv)


def _layer(x, mem, w_in, conv_w, a_log, dt_bias, dn_norm_w, w_mem_kv, w_gate, b_gate,
           w_up_sb, w_up_dn, w_up_mem, w_out, ln_g, ln_b, alpha):
    batch, seq, d = x.shape
    n_mem = mem.shape[1]
    m = batch * seq
    sbw, dnw, memw = w_up_sb.shape[0], w_up_dn.shape[0], w_up_mem.shape[0]
    sbh, dnh = sbw // HEAD_DIM, dnw // HEAD_DIM
    n_main = 4 * sbw + 4 * dnw
    ba0 = n_main
    qm0 = ba0 + 2 * dnh
    assert w_in.shape[1] == qm0 + 2 * memw

    x2 = x.reshape(m, d)
    xb = x2.astype(BF16)
    tm = _tile(m, 1024)

    w_in_b = w_in.astype(BF16)
    col_scale = jnp.concatenate([jnp.full((sbw,), HEAD_DIM ** -0.5 * LOG2E, F32), jnp.ones((n_main - sbw,), F32)])
    h = _tiled_matmul(_proj_scale_body, xb, w_in_b,
                      [(col_scale.reshape(1, n_main), (1, _tile(n_main, 1024)), lambda j, i: (0, j))],
                      BF16, tm, _tile(n_main, 1024), "input_projection", n_cols=n_main)
    pad = (-2 * dnh) % HEAD_DIM
    w_tail = jnp.concatenate([w_in_b[:, qm0:], w_in_b[:, ba0:qm0], jnp.zeros((d, pad), BF16)], axis=1)
    h_tail = _tiled_matmul(_proj_body, xb, w_tail, [], F32, tm, w_tail.shape[1], "input_projection_tail")

    o_sb = _sb_attention(h, batch, seq, sbh, 0)

    ba = h_tail[:, 2 * memw:2 * memw + 2 * dnh].reshape(batch, seq, 2 * dnh)
    ba = ba.transpose(0, 2, 1).reshape(batch, 2 * dnh, seq // DN_CHUNK, DN_CHUNK)
    o_dn = _deltanet(h, conv_w, ba, a_log, dt_bias, dn_norm_w, batch, seq, dnh, 4 * sbh)

    kv = _tiled_matmul(_proj_body, mem.reshape(batch * n_mem, d).astype(BF16), w_mem_kv.astype(BF16), [], BF16,
                       _tile(batch * n_mem, 512), _tile(2 * memw, 1024), "memory_kv_projection")
    o_m = _mem_attention(h_tail, kv, batch, seq, n_mem, memw, _tile(seq, 512))

    n_br = w_gate.shape[0]
    tn = _tile(d, 1024)
    gates = _tiled_matmul(_gate_body, xb, w_gate.astype(BF16),
                          [(b_gate.reshape(1, n_br * d), (1, tn), lambda j, i: (0, j))], BF16, tm, tn, "branch_gates")
    y = _merge(o_sb, o_dn, o_m, w_up_sb.astype(BF16), w_up_dn.astype(BF16), w_up_mem.astype(BF16), gates, tm,
               _tile(d, 512))
    tn_out = _tile(d, 512)
    r = _tiled_matmul(functools.partial(_resid_body, alpha=alpha), y, w_out.astype(BF16),
                      [(x2, (tm, tn_out), lambda j, i: (i, j))], F32, tm, tn_out, "output_projection")
    return _layernorm(r, ln_g, ln_b, _tile(m, 256)).reshape(batch, seq, d)


def kernel(x, mem, w_in, conv_w, a_log, dt_bias, dn_norm_w, w_mem_kv, w_gate, b_gate, w_up_sb, w_up_dn, w_up_mem,
           w_out, ln_g, ln_b):
    depth = w_in.shape[0]
    alpha = (2.0 * depth) ** 0.25
    for l in range(depth):
        x = _layer(x, mem, w_in[l], conv_w[l], a_log[l], dt_bias[l], dn_norm_w[l], w_mem_kv[l], w_gate[l],
                   b_gate[l], w_up_sb[l], w_up_dn[l], w_up_mem[l], w_out[l], ln_g[l], ln_b[l], alpha)
    return x
```

```python
import functools

import numpy as np
import jax
import jax.numpy as jnp
from jax import lax
from jax.experimental import pallas as pl
from jax.experimental.pallas import tpu as pltpu

HEAD_DIM = 128
CONV_K = 4
CONV_HALO = 8
MEM_HEADS = 4
LN_EPS = 1e-5
RMS_EPS = 1e-6
L2_EPS = 1e-6
LOG2E = 1.4426950408889634
SOFTPLUS2_CLAMP = 64.0
MASKED_LOG2_WEIGHT = -1e30
SB_TRI_EXTRA_ROWS = 16
SB_UNDERFLOW_LOG2 = 1100.0

DN_CHUNK = 128
DN_GROUP = 8
SB_BLOCK = 256
SB_HEADS_PER_STEP = 4
V7X_VMEM_LIMIT_BYTES = 48 * 1024 * 1024

F32 = jnp.float32
BF16 = jnp.bfloat16


def _params(n_axes):
    return pltpu.CompilerParams(dimension_semantics=("arbitrary",) * n_axes,
                                vmem_limit_bytes=V7X_VMEM_LIMIT_BYTES)


def _dot(a, b):
    return jnp.dot(a, b, preferred_element_type=F32)


def _dot_nt(a, b):
    return lax.dot_general(a, b, (((1,), (1,)), ((), ())), preferred_element_type=F32)


def _sigmoid(x):
    return 1.0 / (1.0 + jnp.exp(-x))


def _softplus(x):
    return jnp.maximum(x, 0.0) + jnp.log(1.0 + jnp.exp(-jnp.abs(x)))


def _softplus2(x):
    return jnp.maximum(x, jnp.log(1.0 + jnp.exp2(jnp.minimum(x, SOFTPLUS2_CLAMP))) * LOG2E)


def _tile(n, pref):
    t = min(n, pref)
    assert n % t == 0, (n, t)
    return t


def _tiled_matmul(body, a, b, extra, out_dtype, tm, tn, name, n_cols=None):
    m, k = a.shape
    if b.ndim == 3:
        nb = b.shape[2] // tn
        n = b.shape[0] * b.shape[2]
        b_spec = pl.BlockSpec((None, k, tn), lambda j, i: (j // nb, 0, j % nb))
    else:
        n = b.shape[1] if n_cols is None else n_cols
        b_spec = pl.BlockSpec((k, tn), lambda j, i: (0, j))
    assert m % tm == 0 and n % tn == 0
    grid = (n // tn, m // tm)
    in_specs = [pl.BlockSpec((tm, k), lambda j, i: (i, 0)), b_spec]
    in_specs += [pl.BlockSpec(blk, imap) for _, blk, imap in extra]
    return pl.pallas_call(
        body,
        grid=grid,
        in_specs=in_specs,
        out_specs=pl.BlockSpec((tm, tn), lambda j, i: (i, j)),
        out_shape=jax.ShapeDtypeStruct((m, n), out_dtype),
        compiler_params=_params(2),
        name=name,
    )(a, b, *[e[0] for e in extra])


def _proj_scale_body(a_ref, b_ref, s_ref, o_ref):
    o_ref[...] = (_dot(a_ref[...], b_ref[...]) * s_ref[...]).astype(o_ref.dtype)


def _tail_body(x_ref, w_ref, o_ref, xb_ref):
    xb = x_ref[...].astype(BF16)
    xb_ref[...] = xb
    o_ref[...] = _dot(xb, w_ref[...])


def _tail_projection(x2, w_tail, tm):
    m, d = x2.shape
    n = w_tail.shape[1]
    return pl.pallas_call(
        _tail_body,
        grid=(m // tm,),
        in_specs=[pl.BlockSpec((tm, d), lambda i: (i, 0)),
                  pl.BlockSpec((d, n), lambda i: (0, 0), pipeline_mode=pl.Buffered(1))],
        out_specs=[pl.BlockSpec((tm, n), lambda i: (i, 0)), pl.BlockSpec((tm, d), lambda i: (i, 0))],
        out_shape=[jax.ShapeDtypeStruct((m, n), F32), jax.ShapeDtypeStruct((m, d), BF16)],
        compiler_params=_params(1),
        name="input_projection_tail",
    )(x2, w_tail)


def _proj_body(a_ref, b_ref, o_ref):
    o_ref[...] = _dot(a_ref[...], b_ref[...]).astype(o_ref.dtype)


def _resid_body(a_ref, b_ref, x_ref, o_ref, *, alpha):
    o_ref[...] = alpha * x_ref[...] + _dot(a_ref[...], b_ref[...])


def _merge_body(osb_ref, odn_ref, om_ref, wsb_ref, wdn_ref, wm_ref, g0_ref, g1_ref, g2_ref, o_ref):
    y = g0_ref[...].astype(F32) * _dot(osb_ref[...], wsb_ref[...])
    y += g1_ref[...].astype(F32) * _dot(odn_ref[...], wdn_ref[...])
    y += g2_ref[...].astype(F32) * _dot(om_ref[...], wm_ref[...])
    o_ref[...] = y.astype(o_ref.dtype)


def _merge(o_sb, o_dn, o_m, w_sb, w_dn, w_m, gates, tm, tn):
    m = o_sb.shape[0]
    d = w_sb.shape[1]
    row = lambda w: pl.BlockSpec((tm, w), lambda j, i: (i, 0))
    colw = lambda w: pl.BlockSpec((w, tn), lambda j, i: (0, j))
    gate = pl.BlockSpec((tm, tn), lambda j, i: (i, j))
    return pl.pallas_call(
        _merge_body,
        grid=(d // tn, m // tm),
        in_specs=[row(o_sb.shape[1]), row(o_dn.shape[1]), row(o_m.shape[1]),
                  colw(w_sb.shape[0]), colw(w_dn.shape[0]), colw(w_m.shape[0]),
                  gate, gate, gate],
        out_specs=pl.BlockSpec((tm, tn), lambda j, i: (i, j)),
        out_shape=jax.ShapeDtypeStruct((m, d), BF16),
        compiler_params=_params(2),
        name="branch_merge",
    )(o_sb, o_dn, o_m, w_sb, w_dn, w_m, *gates)


def _layernorm_body(r_ref, g_ref, b_ref, o_ref):
    r = r_ref[...]
    mu = jnp.mean(r, axis=-1, keepdims=True)
    rc = r - mu
    var = jnp.mean(rc * rc, axis=-1, keepdims=True)
    o_ref[...] = rc * lax.rsqrt(var + LN_EPS) * g_ref[...] + b_ref[...]


def _layernorm(r, g, b, tm):
    m, d = r.shape
    return pl.pallas_call(
        _layernorm_body,
        grid=(m // tm,),
        in_specs=[pl.BlockSpec((tm, d), lambda i: (i, 0)),
                  pl.BlockSpec((1, d), lambda i: (0, 0)),
                  pl.BlockSpec((1, d), lambda i: (0, 0))],
        out_specs=pl.BlockSpec((tm, d), lambda i: (i, 0)),
        out_shape=jax.ShapeDtypeStruct((m, d), F32),
        compiler_params=_params(1),
        name="post_layernorm",
    )(r, g.reshape(1, d), b.reshape(1, d))


def _sb_body(q_ref, k_ref, v_ref, z_ref, tri_ref, o_ref, vt_ref, acc_ref, lsig_ref, sp_ref, *, tk, nq, hps):
    qi = pl.program_id(2)
    heads = range(hps)
    lanes = lambda hh: slice(hh * HEAD_DIM, (hh + 1) * HEAD_DIM)

    @pl.when(qi == 0)
    def _():
        def transpose_block(c, _):
            blk = v_ref[pl.ds(pl.multiple_of(c * tk, tk), tk), :].astype(F32)
            for hh in heads:
                vt_ref[hh, c] = blk[:, lanes(hh)].T.astype(BF16)
            return 0
        lax.fori_loop(0, nq, transpose_block, 0)

    qs = [q_ref[:, lanes(hh)] for hh in heads]
    key_pos = lax.broadcasted_iota(jnp.int32, (tk, tk), 0)
    qry_pos = lax.broadcasted_iota(jnp.int32, (tk, tk), 1)
    before = key_pos < qry_pos

    def score_dots(j):
        kb = k_ref[pl.ds(pl.multiple_of(j * tk, tk), tk), :]
        return [_dot_nt(kb[:, lanes(hh)], qs[hh]) for hh in heads]

    def scores_store(zts, slot, diagonal):
        for hh in heads:
            sp = _softplus2(zts[hh])
            lsig = zts[hh] - sp
            if diagonal:
                sp = jnp.where(before, sp, 0.0)
                lsig = jnp.where(before, lsig, MASKED_LOG2_WEIGHT)
            lsig_ref[slot, hh] = lsig
            sp_ref[slot, hh] = sp.astype(BF16)

    def suffix_dots(slot):
        return [_dot(tri_ref[...], sp_ref[slot, hh]) for hh in heads]

    def finish(j, slot, sufs, carries):
        out = []
        for hh in heads:
            w = jnp.exp2(lsig_ref[slot, hh] - sufs[hh][:tk] - carries[hh])
            acc_ref[hh] += _dot(vt_ref[hh, j], w.astype(BF16))
            out.append(carries[hh] + sufs[hh][tk:tk + 1])
        return tuple(out)

    def step(state):
        t, carries, _ = state
        slot = t & 1
        sufs = suffix_dots(slot)
        zts = score_dots(qi - t - 1)
        carries = finish(qi - t, slot, sufs, carries)
        scores_store(zts, 1 - slot, False)
        smallest = functools.reduce(jnp.minimum, carries)
        return t + 1, carries, jnp.min(smallest) < SB_UNDERFLOW_LOG2

    acc_ref[...] = jnp.zeros(acc_ref.shape, F32)
    scores_store(score_dots(qi), 0, True)
    t, carries, _ = lax.while_loop(lambda s: jnp.logical_and(s[0] < qi, s[2]), step,
                                   (jnp.int32(0), (jnp.zeros((1, tk), F32),) * hps, jnp.bool_(True)))
    last = t & 1
    finish(qi - t, last, suffix_dots(last), carries)

    for hh in heads:
        z = z_ref[:, lanes(hh)].astype(F32)
        o_ref[:, lanes(hh)] = (acc_ref[hh].T * (z * _sigmoid(z))).astype(o_ref.dtype)


def _sb_attention(h, batch, seq, heads, col0):
    t = _tile(seq, SB_BLOCK)
    nq = seq // t
    hps = SB_HEADS_PER_STEP if heads % SB_HEADS_PER_STEP == 0 and col0 % SB_HEADS_PER_STEP == 0 else 1
    ng = heads // hps
    c0 = col0 // hps
    w = hps * HEAD_DIM
    tri = np.zeros((t + SB_TRI_EXTRA_ROWS, t), np.float32)
    tri[:t] = np.triu(np.ones((t, t), np.float32), 1)
    tri[t] = 1.0
    tri = jnp.asarray(tri, BF16)
    body = functools.partial(_sb_body, tk=t, nq=nq, hps=hps)
    whole_seq = lambda g: pl.BlockSpec((seq, w), lambda b, h_, i: (b, c0 + g * ng + h_), pipeline_mode=pl.Buffered(1))
    return pl.pallas_call(
        body,
        grid=(batch, ng, nq),
        in_specs=[pl.BlockSpec((t, w), lambda b, h_, i: (b * nq + i, c0 + h_)),
                  whole_seq(1), whole_seq(2),
                  pl.BlockSpec((t, w), lambda b, h_, i: (b * nq + i, c0 + 3 * ng + h_)),
                  pl.BlockSpec((t + SB_TRI_EXTRA_ROWS, t), lambda b, h_, i: (0, 0))],
        out_specs=pl.BlockSpec((t, w), lambda b, h_, i: (b * nq + i, h_)),
        out_shape=jax.ShapeDtypeStruct((batch * seq, heads * HEAD_DIM), BF16),
        scratch_shapes=[pltpu.VMEM((hps, nq, HEAD_DIM, t), BF16), pltpu.VMEM((hps, HEAD_DIM, t), F32),
                        pltpu.VMEM((2, hps, t, t), F32), pltpu.VMEM((2, hps, t, t), BF16)],
        compiler_params=_params(3),
        name="stickbreak_attention",
    )(h, h, h, h, tri)


_M_INCL, _M_STRICT, _M_PAIR, _M_LEVEL0 = 0, 1, 2, 3
_N_LEVELS = int(np.log2(DN_CHUNK)) - 1
_M_EYE = _M_LEVEL0 + _N_LEVELS
_M_UPPER = _M_EYE + 1
_N_MASKS = _M_UPPER + 1
_DN_STAGES = 2 * _N_LEVELS + 4
GATE_K_PIECES = _DN_STAGES


def _dn_masks():
    c = DN_CHUNK
    i = np.arange(c)[:, None]
    j = np.arange(c)[None, :]
    masks = [i >= j, i > j, (i % 2 == 1) & (j == i - 1)]
    b = 4
    while b <= c:
        masks.append((i // b == j // b) & (i % b >= b // 2) & (j % b < b // 2))
        b *= 2
    masks += [i == j, i <= j]
    assert len(masks) == _N_MASKS
    return np.stack(masks).astype(np.float32)


def _split3(x):
    p1 = x.astype(BF16)
    r = x - p1.astype(F32)
    p2 = r.astype(BF16)
    p3 = (r - p2.astype(F32)).astype(BF16)
    return p1, p2, p3


def _dn_body(q_ref, k_ref, v_ref, z_ref, cq_ref, ck_ref, cv_ref, b_ref, a_ref, alog_ref, dtb_ref, nw_ref, msk_ref,
             x_ref, wg0_ref, wg1_ref, wg2_ref, bg0_ref, bg1_ref, bg2_ref,
             o_ref, g0_ref, g1_ref, g2_ref,
             gc_ref, beta_ref, gl_ref, xf_ref, lhs_ref, add_ref, state_ref, acc_ref, *, n_groups):
    c = DN_CHUNK
    rows = DN_GROUP * c
    g = pl.program_id(2)
    wg_refs, bg_refs, gate_refs = (wg0_ref, wg1_ref, wg2_ref), (bg0_ref, bg1_ref, bg2_ref), (g0_ref, g1_ref, g2_ref)
    k_piece = x_ref.shape[1] // GATE_K_PIECES

    @pl.when(g == 0)
    def _():
        gd = -jnp.exp(alog_ref[0]) * _softplus(a_ref[0, 0] + dtb_ref[0])
        upper = msk_ref[_M_UPPER].astype(BF16)
        ones = jnp.ones((c, c), BF16)
        pieces = _split3(gd)
        gc_ref[...] = sum(_dot(p, upper) for p in pieces)
        gl_ref[...] = sum(_dot(p, ones) for p in pieces)
        beta_ref[...] = _sigmoid(b_ref[0, 0])
        state_ref[...] = jnp.zeros(state_ref.shape, F32)

    def gate_piece(i):
        ks = slice(i * k_piece, (i + 1) * k_piece)
        for t in range(3):
            part = _dot(x_ref[:, ks], wg_refs[t][ks, :])
            if i == 0:
                acc_ref[t] = part
            elif i < GATE_K_PIECES - 1:
                acc_ref[t] += part
            else:
                gate_refs[t][...] = _sigmoid(acc_ref[t] + part + bg_refs[t][...]).astype(gate_refs[t].dtype)

    def conv_silu(idx, x_blk_ref, cw_ref, first):
        if first:
            xf_ref[idx, 0:CONV_HALO, :] = jnp.zeros((CONV_HALO, HEAD_DIM), F32)
        else:
            xf_ref[idx, 0:CONV_HALO, :] = xf_ref[idx, rows:rows + CONV_HALO, :]
        xf_ref[idx, CONV_HALO:, :] = x_blk_ref[...].astype(F32)
        cw = cw_ref[...]
        y = None
        for s in range(CONV_K):
            tap = xf_ref[idx, CONV_HALO - s:CONV_HALO - s + rows, :] * cw[CONV_K - 1 - s:CONV_K - s, :]
            y = tap if y is None else y + tap
        return y * _sigmoid(y)

    def l2norm(t):
        return t * lax.rsqrt(jnp.sum(t * t, axis=-1, keepdims=True) + L2_EPS)

    def local_work(first, per_stage):
        stage = [0]

        def stage_done():
            for fn in per_stage[stage[0]]:
                fn()
            stage[0] += 1

        slot = g & 1
        qg = l2norm(conv_silu(0, q_ref, cq_ref, first)) * (HEAD_DIM ** -0.5)
        kg = l2norm(conv_silu(1, k_ref, ck_ref, first))
        vg = conv_silu(2, v_ref, cv_ref, first)
        cs = range(DN_GROUP)
        sl = lambda ci: slice(ci * c, (ci + 1) * c)
        qs, ks, vs = [qg[sl(ci)] for ci in cs], [kg[sl(ci)] for ci in cs], [vg[sl(ci)] for ci in cs]
        gls, gc_cols, beta_cols, decays = [], [], [], []
        for ci in cs:
            chunk = g * DN_GROUP + ci
            gc_row = jnp.broadcast_to(gc_ref[pl.ds(chunk, 1), :], (c, c))
            gc_col = gc_row.T
            incl = msk_ref[_M_INCL]
            gls.append(gl_ref[pl.ds(chunk, 1), :])
            gc_cols.append(gc_col)
            beta_cols.append(jnp.broadcast_to(beta_ref[pl.ds(chunk, 1), :], (c, c)).T)
            decays.append(jnp.exp((gc_col - gc_row) * incl) * incl)
        kbs = [ks[ci] * beta_cols[ci] for ci in cs]
        prods = [_dot_nt(jnp.concatenate([qs[ci], kbs[ci]], axis=0).astype(BF16), ks[ci].astype(BF16)) for ci in cs]
        stage_done()
        a_ins = [prods[ci][:c] * decays[ci] for ci in cs]
        a_los = [prods[ci][c:] * decays[ci] * msk_ref[_M_STRICT] for ci in cs]
        t_invs = [msk_ref[_M_EYE] - a_lo * msk_ref[_M_PAIR] for a_lo in a_los]
        for lvl in range(_N_LEVELS):
            offs = [(a_lo * msk_ref[_M_LEVEL0 + lvl]).astype(BF16) for a_lo in a_los]
            tbs = [t_inv.astype(BF16) for t_inv in t_invs]
            xs = [_dot(tbs[ci], offs[ci]).astype(BF16) for ci in cs]
            stage_done()
            t_invs = [t_invs[ci] - _dot(xs[ci], tbs[ci]) for ci in cs]
            stage_done()
        eg_cols = [jnp.exp(gc_col) for gc_col in gc_cols]
        rhss = [jnp.concatenate([kbs[ci] * eg_cols[ci], vs[ci] * beta_cols[ci]], axis=1).astype(BF16) for ci in cs]
        wus = [_dot(t_invs[ci].astype(BF16), rhss[ci]).astype(BF16) for ci in cs]
        stage_done()
        k_ends = [ks[ci] * jnp.exp(gls[ci] - gc_cols[ci]) for ci in cs]
        pns = [_dot(k_ends[ci].T.astype(BF16), wus[ci]) for ci in cs]
        stage_done()
        aw_aus = [_dot(a_ins[ci].astype(BF16), wus[ci]) for ci in cs]
        stage_done()
        assert stage[0] == _DN_STAGES == len(per_stage)
        for ci in cs:
            q_eff = qs[ci] * eg_cols[ci] - aw_aus[ci][:, :HEAD_DIM]
            lhs_ref[slot, ci] = jnp.concatenate([q_eff, pns[ci][:, :HEAD_DIM]], axis=0).astype(BF16)
            add_ref[slot, ci] = jnp.concatenate([aw_aus[ci][:, HEAD_DIM:], pns[ci][:, HEAD_DIM:]], axis=0)

    def scan_chunk(slot, ci, state):
        both = _dot(lhs_ref[slot, ci], state.astype(BF16))
        o = both[:c] + add_ref[slot, ci, :c]
        gl = gl_ref[pl.ds((g - 1) * DN_GROUP + ci, 1), :]
        state = state * jnp.exp(gl) - both[c:] + add_ref[slot, ci, c:]
        o = o * lax.rsqrt(jnp.mean(o * o, axis=-1, keepdims=True) + RMS_EPS) * nw_ref[...]
        zr = z_ref[ci * c:(ci + 1) * c, :].astype(F32)
        o_ref[ci * c:(ci + 1) * c, :] = (o * (zr * _sigmoid(zr))).astype(o_ref.dtype)
        return state

    def run(first, with_scan):
        box = [state_ref[...]] if with_scan else None

        def scan_hook(ci):
            box[0] = scan_chunk(1 - (g & 1), ci, box[0])

        per_stage = [[functools.partial(gate_piece, i)] for i in range(_DN_STAGES)]
        if with_scan:
            for ci in range(DN_GROUP):
                per_stage[2 * ci + 1].append(functools.partial(scan_hook, ci))
        local_work(first, per_stage)
        if with_scan:
            state_ref[...] = box[0]

    @pl.when(g == 0)
    def _():
        run(True, False)

    @pl.when(jnp.logical_and(g > 0, g < n_groups))
    def _():
        run(False, True)

    @pl.when(g == n_groups)
    def _():
        state = state_ref[...]
        for ci in range(DN_GROUP):
            state = scan_chunk((n_groups - 1) & 1, ci, state)


def _deltanet_and_gates(h, conv_w, ba, a_log, dt_bias, norm_w, xb, w_gate, b_gate, batch, seq, heads, col0):
    c = DN_CHUNK
    rows = DN_GROUP * c
    assert seq % rows == 0 and w_gate.shape[0] == 3
    n_chunks, ng = seq // c, seq // rows
    m, d = xb.shape
    gw = d // heads
    assert gw * heads == d and gw % HEAD_DIM == 0 and d % GATE_K_PIECES == 0
    masks = jnp.asarray(_dn_masks())
    lane = lambda v: jnp.broadcast_to(v.astype(F32)[:, None, None], (heads, 1, c))
    body = functools.partial(_dn_body, n_groups=ng)
    cur = lambda b, g: b * ng + jnp.minimum(g, ng - 1)
    prev = lambda b, g: b * ng + jnp.maximum(g - 1, 0)
    hcol = lambda t: pl.BlockSpec((rows, HEAD_DIM), lambda b, h_, g: (cur(b, g), col0 + t * heads + h_))
    cw = lambda t: pl.BlockSpec((CONV_K, HEAD_DIM), lambda b, h_, g: (0, t * heads + h_))
    wgs = lambda t: pl.BlockSpec((None, d, gw), lambda b, h_, g: (t, 0, h_))
    bgs = lambda t: pl.BlockSpec((None, 1, gw), lambda b, h_, g: (t, 0, h_))
    b_gate = b_gate.reshape(3, 1, d)
    gate_out = pl.BlockSpec((rows, gw), lambda b, h_, g: (cur(b, g), h_))
    outs = pl.pallas_call(
        body,
        grid=(batch, heads, ng + 1),
        in_specs=[hcol(0), hcol(1), hcol(2),
                  pl.BlockSpec((rows, HEAD_DIM), lambda b, h_, g: (prev(b, g), col0 + 3 * heads + h_)),
                  cw(0), cw(1), cw(2),
                  pl.BlockSpec((1, 1, n_chunks, c), lambda b, h_, g: (b, h_, 0, 0)),
                  pl.BlockSpec((1, 1, n_chunks, c), lambda b, h_, g: (b, heads + h_, 0, 0)),
                  pl.BlockSpec((1, 1, c), lambda b, h_, g: (h_, 0, 0)),
                  pl.BlockSpec((1, 1, c), lambda b, h_, g: (h_, 0, 0)),
                  pl.BlockSpec((1, HEAD_DIM), lambda b, h_, g: (0, 0)),
                  pl.BlockSpec((_N_MASKS, c, c), lambda b, h_, g: (0, 0, 0)),
                  pl.BlockSpec((rows, d), lambda b, h_, g: (cur(b, g), 0)),
                  wgs(0), wgs(1), wgs(2), bgs(0), bgs(1), bgs(2)],
        out_specs=[pl.BlockSpec((rows, HEAD_DIM), lambda b, h_, g: (prev(b, g), h_)), gate_out, gate_out, gate_out],
        out_shape=[jax.ShapeDtypeStruct((m, heads * HEAD_DIM), BF16)] + [jax.ShapeDtypeStruct((m, d), BF16)] * 3,
        scratch_shapes=[pltpu.VMEM((n_chunks, c), F32)] * 3
        + [pltpu.VMEM((3, CONV_HALO + rows, HEAD_DIM), F32),
           pltpu.VMEM((2, DN_GROUP, 2 * c, HEAD_DIM), BF16), pltpu.VMEM((2, DN_GROUP, 2 * c, HEAD_DIM), F32),
           pltpu.VMEM((HEAD_DIM, HEAD_DIM), F32), pltpu.VMEM((3, rows, gw), F32)],
        compiler_params=_params(3),
        name="deltanet_and_gates",
    )(h, h, h, h, conv_w, conv_w, conv_w, ba, ba, lane(a_log), lane(dt_bias), norm_w.reshape(1, HEAD_DIM), masks,
      xb, w_gate, w_gate, w_gate, b_gate, b_gate, b_gate)
    return outs[0], outs[1:]


def _mem_body(q_ref, z_ref, kv_ref, o_ref, *, width):
    for hh in range(width // HEAD_DIM):
        sl = slice(hh * HEAD_DIM, (hh + 1) * HEAD_DIM)
        q = (q_ref[:, sl] * (HEAD_DIM ** -0.5)).astype(BF16)
        k = kv_ref[:, sl]
        v = kv_ref[:, width + hh * HEAD_DIM:width + (hh + 1) * HEAD_DIM]
        s = _dot_nt(q, k)
        p = jnp.exp(s - jnp.max(s, axis=-1, keepdims=True))
        o = _dot(p.astype(BF16), v) * (1.0 / jnp.sum(p, axis=-1, keepdims=True))
        z = z_ref[:, sl]
        o_ref[:, sl] = (o * (z * _sigmoid(z))).astype(o_ref.dtype)


def _mem_attention(h_tail, kv, batch, seq, n_mem, width, tq):
    nq = seq // tq
    body = functools.partial(_mem_body, width=width)
    return pl.pallas_call(
        body,
        grid=(batch, nq),
        in_specs=[pl.BlockSpec((tq, width), lambda b, i: (b * nq + i, 0)),
                  pl.BlockSpec((tq, width), lambda b, i: (b * nq + i, 1)),
                  pl.BlockSpec((n_mem, 2 * width), lambda b, i: (b, 0))],
        out_specs=pl.BlockSpec((tq, width), lambda b, i: (b * nq + i, 0)),
        out_shape=jax.ShapeDtypeStruct((batch * seq, width), BF16),
        compiler_params=_params(2),
        name="memory_attention",
    )(h_tail, h_tail, kv)


def _layer(x, mem, w_in, conv_w, a_log, dt_bias, dn_norm_w, w_mem_kv, w_gate, b_gate,
           w_up_sb, w_up_dn, w_up_mem, w_out, ln_g, ln_b, alpha):
    batch, seq, d = x.shape
    n_mem = mem.shape[1]
    m = batch * seq
    sbw, dnw, memw = w_up_sb.shape[0], w_up_dn.shape[0], w_up_mem.shape[0]
    sbh, dnh = sbw // HEAD_DIM, dnw // HEAD_DIM
    n_main = 4 * sbw + 4 * dnw
    ba0 = n_main
    qm0 = ba0 + 2 * dnh
    assert w_in.shape[1] == qm0 + 2 * memw

    x2 = x.reshape(m, d)
    tm = _tile(m, 1024)

    w_in_b = w_in.astype(BF16)
    pad = (-2 * dnh) % HEAD_DIM
    w_tail = jnp.concatenate([w_in_b[:, qm0:], w_in_b[:, ba0:qm0], jnp.zeros((d, pad), BF16)], axis=1)
    h_tail, xb = _tail_projection(x2, w_tail, _tile(m, 512))
    col_scale = jnp.concatenate([jnp.full((sbw,), HEAD_DIM ** -0.5 * LOG2E, F32), jnp.ones((n_main - sbw,), F32)])
    h = _tiled_matmul(_proj_scale_body, xb, w_in_b,
                      [(col_scale.reshape(1, n_main), (1, _tile(n_main, 1024)), lambda j, i: (0, j))],
                      BF16, tm, _tile(n_main, 1024), "input_projection", n_cols=n_main)

    o_sb = _sb_attention(h, batch, seq, sbh, 0)

    ba = h_tail[:, 2 * memw:2 * memw + 2 * dnh].reshape(batch, seq, 2 * dnh)
    ba = ba.transpose(0, 2, 1).reshape(batch, 2 * dnh, seq // DN_CHUNK, DN_CHUNK)
    o_dn, gates = _deltanet_and_gates(h, conv_w, ba, a_log, dt_bias, dn_norm_w, xb, w_gate.astype(BF16), b_gate,
                                      batch, seq, dnh, 4 * sbh)

    kv = _tiled_matmul(_proj_body, mem.reshape(batch * n_mem, d).astype(BF16), w_mem_kv.astype(BF16), [], BF16,
                       _tile(batch * n_mem, 512), _tile(2 * memw, 1024), "memory_kv_projection")
    o_m = _mem_attention(h_tail, kv, batch, seq, n_mem, memw, _tile(seq, 512))

    y = _merge(o_sb, o_dn, o_m, w_up_sb.astype(BF16), w_up_dn.astype(BF16), w_up_mem.astype(BF16), gates, tm,
               _tile(d, 512))
    tn_out = _tile(d, 512)
    r = _tiled_matmul(functools.partial(_resid_body, alpha=alpha), y, w_out.astype(BF16),
                      [(x2, (tm, tn_out), lambda j, i: (i, j))], F32, tm, tn_out, "output_projection")
    return _layernorm(r, ln_g, ln_b, _tile(m, 256)).reshape(batch, seq, d)


def kernel(x, mem, w_in, conv_w, a_log, dt_bias, dn_norm_w, w_mem_kv, w_gate, b_gate, w_up_sb, w_up_dn, w_up_mem,
           w_out, ln_g, ln_b):
    depth = w_in.shape[0]
    alpha = (2.0 * depth) ** 0.25
    for l in range(depth):
        x = _layer(x, mem, w_in[l], conv_w[l], a_log[l], dt_bias[l], dn_norm_w[l], w_mem_kv[l], w_gate[l],
                   b_gate[l], w_up_sb[l], w_up_dn[l], w_up_mem[l], w_out[l], ln_g[l], ln_b[l], alpha)
    return x
```

```python
import functools

import numpy as np
import jax
import jax.numpy as jnp
from jax import lax
from jax.experimental import pallas as pl
from jax.experimental.pallas import tpu as pltpu

HEAD_DIM = 128
CONV_K = 4
CONV_HALO = 8
MEM_HEADS = 4
LN_EPS = 1e-5
RMS_EPS = 1e-6
L2_EPS = 1e-6
LOG2E = 1.4426950408889634
SOFTPLUS2_CLAMP = 64.0
MASKED_LOG2_WEIGHT = -1e30
SB_TRI_EXTRA_ROWS = 16
SB_UNDERFLOW_LOG2 = 1100.0

DN_CHUNK = 128
DN_GROUP = 8
SB_BLOCK = 256
SB_HEADS_PER_STEP = 4
V7X_VMEM_LIMIT_BYTES = 48 * 1024 * 1024

F32 = jnp.float32
BF16 = jnp.bfloat16


def _params(n_axes):
    return pltpu.CompilerParams(dimension_semantics=("arbitrary",) * n_axes,
                                vmem_limit_bytes=V7X_VMEM_LIMIT_BYTES)


def _dot(a, b):
    return jnp.dot(a, b, preferred_element_type=F32)


def _dot_nt(a, b):
    return lax.dot_general(a, b, (((1,), (1,)), ((), ())), preferred_element_type=F32)


def _sigmoid(x):
    return 1.0 / (1.0 + jnp.exp(-x))


def _softplus(x):
    return jnp.maximum(x, 0.0) + jnp.log(1.0 + jnp.exp(-jnp.abs(x)))


def _softplus2(x):
    return jnp.maximum(x, jnp.log(1.0 + jnp.exp2(jnp.minimum(x, SOFTPLUS2_CLAMP))) * LOG2E)


def _tile(n, pref):
    t = min(n, pref)
    assert n % t == 0, (n, t)
    return t


def _tiled_matmul(body, a, b, extra, out_dtype, tm, tn, name, n_cols=None):
    m, k = a.shape
    if b.ndim == 3:
        nb = b.shape[2] // tn
        n = b.shape[0] * b.shape[2]
        b_spec = pl.BlockSpec((None, k, tn), lambda j, i: (j // nb, 0, j % nb))
    else:
        n = b.shape[1] if n_cols is None else n_cols
        b_spec = pl.BlockSpec((k, tn), lambda j, i: (0, j))
    assert m % tm == 0 and n % tn == 0
    grid = (n // tn, m // tm)
    in_specs = [pl.BlockSpec((tm, k), lambda j, i: (i, 0)), b_spec]
    in_specs += [pl.BlockSpec(blk, imap) for _, blk, imap in extra]
    return pl.pallas_call(
        body,
        grid=grid,
        in_specs=in_specs,
        out_specs=pl.BlockSpec((tm, tn), lambda j, i: (i, j)),
        out_shape=jax.ShapeDtypeStruct((m, n), out_dtype),
        compiler_params=_params(2),
        name=name,
    )(a, b, *[e[0] for e in extra])


def _proj_scale_body(a_ref, b_ref, s_ref, o_ref):
    o_ref[...] = (_dot(a_ref[...], b_ref[...]) * s_ref[...]).astype(o_ref.dtype)


def _tail_body(x_ref, w_ref, o_ref, xb_ref):
    xb = x_ref[...].astype(BF16)
    xb_ref[...] = xb
    o_ref[...] = _dot(xb, w_ref[...])


def _tail_projection(x2, w_tail, tm):
    m, d = x2.shape
    n = w_tail.shape[1]
    return pl.pallas_call(
        _tail_body,
        grid=(m // tm,),
        in_specs=[pl.BlockSpec((tm, d), lambda i: (i, 0)),
                  pl.BlockSpec((d, n), lambda i: (0, 0), pipeline_mode=pl.Buffered(1))],
        out_specs=[pl.BlockSpec((tm, n), lambda i: (i, 0)), pl.BlockSpec((tm, d), lambda i: (i, 0))],
        out_shape=[jax.ShapeDtypeStruct((m, n), F32), jax.ShapeDtypeStruct((m, d), BF16)],
        compiler_params=_params(1),
        name="input_projection_tail",
    )(x2, w_tail)


def _proj_body(a_ref, b_ref, o_ref):
    o_ref[...] = _dot(a_ref[...], b_ref[...]).astype(o_ref.dtype)


def _resid_body(a_ref, b_ref, x_ref, o_ref, *, alpha):
    o_ref[...] = alpha * x_ref[...] + _dot(a_ref[...], b_ref[...])


def _merge_body(osb_ref, odn_ref, om_ref, wsb_ref, wdn_ref, wm_ref, g0_ref, g1_ref, g2_ref, o_ref):
    y = g0_ref[...].astype(F32) * _dot(osb_ref[...], wsb_ref[...])
    y += g1_ref[...].astype(F32) * _dot(odn_ref[...], wdn_ref[...])
    y += g2_ref[...].astype(F32) * _dot(om_ref[...], wm_ref[...])
    o_ref[...] = y.astype(o_ref.dtype)


def _merge(o_sb, o_dn, o_m, w_sb, w_dn, w_m, gates, tm, tn):
    m = o_sb.shape[0]
    d = w_sb.shape[1]
    row = lambda w: pl.BlockSpec((tm, w), lambda j, i: (i, 0))
    colw = lambda w: pl.BlockSpec((w, tn), lambda j, i: (0, j))
    gate = pl.BlockSpec((tm, tn), lambda j, i: (i, j))
    return pl.pallas_call(
        _merge_body,
        grid=(d // tn, m // tm),
        in_specs=[row(o_sb.shape[1]), row(o_dn.shape[1]), row(o_m.shape[1]),
                  colw(w_sb.shape[0]), colw(w_dn.shape[0]), colw(w_m.shape[0]),
                  gate, gate, gate],
        out_specs=pl.BlockSpec((tm, tn), lambda j, i: (i, j)),
        out_shape=jax.ShapeDtypeStruct((m, d), BF16),
        compiler_params=_params(2),
        name="branch_merge",
    )(o_sb, o_dn, o_m, w_sb, w_dn, w_m, *gates)


def _layernorm_body(r_ref, g_ref, b_ref, o_ref):
    r = r_ref[...]
    mu = jnp.mean(r, axis=-1, keepdims=True)
    rc = r - mu
    var = jnp.mean(rc * rc, axis=-1, keepdims=True)
    o_ref[...] = rc * lax.rsqrt(var + LN_EPS) * g_ref[...] + b_ref[...]


def _layernorm(r, g, b, tm):
    m, d = r.shape
    return pl.pallas_call(
        _layernorm_body,
        grid=(m // tm,),
        in_specs=[pl.BlockSpec((tm, d), lambda i: (i, 0)),
                  pl.BlockSpec((1, d), lambda i: (0, 0)),
                  pl.BlockSpec((1, d), lambda i: (0, 0))],
        out_specs=pl.BlockSpec((tm, d), lambda i: (i, 0)),
        out_shape=jax.ShapeDtypeStruct((m, d), F32),
        compiler_params=_params(1),
        name="post_layernorm",
    )(r, g.reshape(1, d), b.reshape(1, d))


def _sb_body(q_ref, k_ref, v_ref, z_ref, tri_ref, o_ref, vt_ref, acc_ref, lsig_ref, sp_ref, *, tk, nq, hps):
    qi = pl.program_id(2)
    heads = range(hps)
    lanes = lambda hh: slice(hh * HEAD_DIM, (hh + 1) * HEAD_DIM)

    @pl.when(qi == 0)
    def _():
        def transpose_block(c, _):
            blk = v_ref[pl.ds(pl.multiple_of(c * tk, tk), tk), :].astype(F32)
            for hh in heads:
                vt_ref[hh, c] = blk[:, lanes(hh)].T.astype(BF16)
            return 0
        lax.fori_loop(0, nq, transpose_block, 0)

    qs = [q_ref[:, lanes(hh)] for hh in heads]
    key_pos = lax.broadcasted_iota(jnp.int32, (tk, tk), 0)
    qry_pos = lax.broadcasted_iota(jnp.int32, (tk, tk), 1)
    before = key_pos < qry_pos

    def score_dots(j):
        kb = k_ref[pl.ds(pl.multiple_of(j * tk, tk), tk), :]
        return [_dot_nt(kb[:, lanes(hh)], qs[hh]) for hh in heads]

    def scores_store(zts, slot, diagonal):
        for hh in heads:
            sp = _softplus2(zts[hh])
            lsig = zts[hh] - sp
            if diagonal:
                sp = jnp.where(before, sp, 0.0)
                lsig = jnp.where(before, lsig, MASKED_LOG2_WEIGHT)
            lsig_ref[slot, hh] = lsig
            sp_ref[slot, hh] = sp.astype(BF16)

    def suffix_dots(slot):
        return [_dot(tri_ref[...], sp_ref[slot, hh]) for hh in heads]

    def finish(j, slot, sufs, carries):
        out = []
        for hh in heads:
            w = jnp.exp2(lsig_ref[slot, hh] - sufs[hh][:tk] - carries[hh])
            acc_ref[hh] += _dot(vt_ref[hh, j], w.astype(BF16))
            out.append(carries[hh] + sufs[hh][tk:tk + 1])
        return tuple(out)

    def step(state):
        t, carries, _ = state
        slot = t & 1
        sufs = suffix_dots(slot)
        zts = score_dots(qi - t - 1)
        carries = finish(qi - t, slot, sufs, carries)
        scores_store(zts, 1 - slot, False)
        smallest = functools.reduce(jnp.minimum, carries)
        return t + 1, carries, jnp.min(smallest) < SB_UNDERFLOW_LOG2

    acc_ref[...] = jnp.zeros(acc_ref.shape, F32)
    scores_store(score_dots(qi), 0, True)
    t, carries, _ = lax.while_loop(lambda s: jnp.logical_and(s[0] < qi, s[2]), step,
                                   (jnp.int32(0), (jnp.zeros((1, tk), F32),) * hps, jnp.bool_(True)))
    last = t & 1
    finish(qi - t, last, suffix_dots(last), carries)

    for hh in heads:
        z = z_ref[:, lanes(hh)].astype(F32)
        o_ref[:, lanes(hh)] = (acc_ref[hh].T * (z * _sigmoid(z))).astype(o_ref.dtype)


def _sb_attention(h, batch, seq, heads, col0):
    t = _tile(seq, SB_BLOCK)
    nq = seq // t
    hps = SB_HEADS_PER_STEP if heads % SB_HEADS_PER_STEP == 0 and col0 % SB_HEADS_PER_STEP == 0 else 1
    ng = heads // hps
    c0 = col0 // hps
    w = hps * HEAD_DIM
    tri = np.zeros((t + SB_TRI_EXTRA_ROWS, t), np.float32)
    tri[:t] = np.triu(np.ones((t, t), np.float32), 1)
    tri[t] = 1.0
    tri = jnp.asarray(tri, BF16)
    body = functools.partial(_sb_body, tk=t, nq=nq, hps=hps)
    whole_seq = lambda g: pl.BlockSpec((seq, w), lambda b, h_, i: (b, c0 + g * ng + h_), pipeline_mode=pl.Buffered(1))
    return pl.pallas_call(
        body,
        grid=(batch, ng, nq),
        in_specs=[pl.BlockSpec((t, w), lambda b, h_, i: (b * nq + i, c0 + h_)),
                  whole_seq(1), whole_seq(2),
                  pl.BlockSpec((t, w), lambda b, h_, i: (b * nq + i, c0 + 3 * ng + h_)),
                  pl.BlockSpec((t + SB_TRI_EXTRA_ROWS, t), lambda b, h_, i: (0, 0))],
        out_specs=pl.BlockSpec((t, w), lambda b, h_, i: (b * nq + i, h_)),
        out_shape=jax.ShapeDtypeStruct((batch * seq, heads * HEAD_DIM), BF16),
        scratch_shapes=[pltpu.VMEM((hps, nq, HEAD_DIM, t), BF16), pltpu.VMEM((hps, HEAD_DIM, t), F32),
                        pltpu.VMEM((2, hps, t, t), F32), pltpu.VMEM((2, hps, t, t), BF16)],
        compiler_params=_params(3),
        name="stickbreak_attention",
    )(h, h, h, h, tri)


_M_INCL, _M_STRICT, _M_PAIR, _M_LEVEL0 = 0, 1, 2, 3
_N_LEVELS = int(np.log2(DN_CHUNK)) - 1
_M_EYE = _M_LEVEL0 + _N_LEVELS
_M_UPPER = _M_EYE + 1
_N_MASKS = _M_UPPER + 1
_DN_STAGES = 2 * _N_LEVELS + 4
GATE_K_PIECES = 4
assert _DN_STAGES % GATE_K_PIECES == 0


def _dn_masks():
    c = DN_CHUNK
    i = np.arange(c)[:, None]
    j = np.arange(c)[None, :]
    masks = [i >= j, i > j, (i % 2 == 1) & (j == i - 1)]
    b = 4
    while b <= c:
        masks.append((i // b == j // b) & (i % b >= b // 2) & (j % b < b // 2))
        b *= 2
    masks += [i == j, i <= j]
    assert len(masks) == _N_MASKS
    return np.stack(masks).astype(np.float32)


def _split3(x):
    p1 = x.astype(BF16)
    r = x - p1.astype(F32)
    p2 = r.astype(BF16)
    p3 = (r - p2.astype(F32)).astype(BF16)
    return p1, p2, p3


def _dn_body(q_ref, k_ref, v_ref, z_ref, cq_ref, ck_ref, cv_ref, b_ref, a_ref, alog_ref, dtb_ref, nw_ref, msk_ref,
             x_ref, wg0_ref, wg1_ref, wg2_ref, bg0_ref, bg1_ref, bg2_ref,
             o_ref, g0_ref, g1_ref, g2_ref,
             gc_ref, beta_ref, gl_ref, xf_ref, lhs_ref, add_ref, state_ref, acc_ref, *, n_groups):
    c = DN_CHUNK
    rows = DN_GROUP * c
    g = pl.program_id(2)
    wg_refs, bg_refs, gate_refs = (wg0_ref, wg1_ref, wg2_ref), (bg0_ref, bg1_ref, bg2_ref), (g0_ref, g1_ref, g2_ref)
    k_piece = x_ref.shape[1] // GATE_K_PIECES

    @pl.when(g == 0)
    def _():
        gd = -jnp.exp(alog_ref[0]) * _softplus(a_ref[0, 0] + dtb_ref[0])
        upper = msk_ref[_M_UPPER].astype(BF16)
        ones = jnp.ones((c, c), BF16)
        pieces = _split3(gd)
        gc_ref[...] = sum(_dot(p, upper) for p in pieces)
        gl_ref[...] = sum(_dot(p, ones) for p in pieces)
        beta_ref[...] = _sigmoid(b_ref[0, 0])
        state_ref[...] = jnp.zeros(state_ref.shape, F32)

    def gate_piece(i):
        ks = slice(i * k_piece, (i + 1) * k_piece)
        for t in range(3):
            part = _dot(x_ref[:, ks], wg_refs[t][ks, :])
            if i == 0:
                acc_ref[t] = part
            elif i < GATE_K_PIECES - 1:
                acc_ref[t] += part
            else:
                gate_refs[t][...] = _sigmoid(acc_ref[t] + part + bg_refs[t][...]).astype(gate_refs[t].dtype)

    def conv_silu(idx, x_blk_ref, cw_ref, first):
        if first:
            xf_ref[idx, 0:CONV_HALO, :] = jnp.zeros((CONV_HALO, HEAD_DIM), F32)
        else:
            xf_ref[idx, 0:CONV_HALO, :] = xf_ref[idx, rows:rows + CONV_HALO, :]
        xf_ref[idx, CONV_HALO:, :] = x_blk_ref[...].astype(F32)
        cw = cw_ref[...]
        y = None
        for s in range(CONV_K):
            tap = xf_ref[idx, CONV_HALO - s:CONV_HALO - s + rows, :] * cw[CONV_K - 1 - s:CONV_K - s, :]
            y = tap if y is None else y + tap
        return y * _sigmoid(y)

    def l2norm(t):
        return t * lax.rsqrt(jnp.sum(t * t, axis=-1, keepdims=True) + L2_EPS)

    def local_work(first, per_stage):
        stage = [0]

        def stage_done():
            for fn in per_stage[stage[0]]:
                fn()
            stage[0] += 1

        slot = g & 1
        qg = l2norm(conv_silu(0, q_ref, cq_ref, first)) * (HEAD_DIM ** -0.5)
        kg = l2norm(conv_silu(1, k_ref, ck_ref, first))
        vg = conv_silu(2, v_ref, cv_ref, first)
        cs = range(DN_GROUP)
        sl = lambda ci: slice(ci * c, (ci + 1) * c)
        qs, ks, vs = [qg[sl(ci)] for ci in cs], [kg[sl(ci)] for ci in cs], [vg[sl(ci)] for ci in cs]
        gls, gc_cols, beta_cols, decays = [], [], [], []
        for ci in cs:
            chunk = g * DN_GROUP + ci
            gc_row = jnp.broadcast_to(gc_ref[pl.ds(chunk, 1), :], (c, c))
            gc_col = gc_row.T
            incl = msk_ref[_M_INCL]
            gls.append(gl_ref[pl.ds(chunk, 1), :])
            gc_cols.append(gc_col)
            beta_cols.append(jnp.broadcast_to(beta_ref[pl.ds(chunk, 1), :], (c, c)).T)
            decays.append(jnp.exp((gc_col - gc_row) * incl) * incl)
        kbs = [ks[ci] * beta_cols[ci] for ci in cs]
        prods = [_dot_nt(jnp.concatenate([qs[ci], kbs[ci]], axis=0).astype(BF16), ks[ci].astype(BF16)) for ci in cs]
        stage_done()
        a_ins = [prods[ci][:c] * decays[ci] for ci in cs]
        a_los = [prods[ci][c:] * decays[ci] * msk_ref[_M_STRICT] for ci in cs]
        t_invs = [msk_ref[_M_EYE] - a_lo * msk_ref[_M_PAIR] for a_lo in a_los]
        for lvl in range(_N_LEVELS):
            offs = [(a_lo * msk_ref[_M_LEVEL0 + lvl]).astype(BF16) for a_lo in a_los]
            tbs = [t_inv.astype(BF16) for t_inv in t_invs]
            xs = [_dot(tbs[ci], offs[ci]).astype(BF16) for ci in cs]
            stage_done()
            t_invs = [t_invs[ci] - _dot(xs[ci], tbs[ci]) for ci in cs]
            stage_done()
        eg_cols = [jnp.exp(gc_col) for gc_col in gc_cols]
        rhss = [jnp.concatenate([kbs[ci] * eg_cols[ci], vs[ci] * beta_cols[ci]], axis=1).astype(BF16) for ci in cs]
        wus = [_dot(t_invs[ci].astype(BF16), rhss[ci]).astype(BF16) for ci in cs]
        stage_done()
        k_ends = [ks[ci] * jnp.exp(gls[ci] - gc_cols[ci]) for ci in cs]
        pns = [_dot(k_ends[ci].T.astype(BF16), wus[ci]) for ci in cs]
        stage_done()
        aw_aus = [_dot(a_ins[ci].astype(BF16), wus[ci]) for ci in cs]
        stage_done()
        assert stage[0] == _DN_STAGES == len(per_stage)
        for ci in cs:
            q_eff = qs[ci] * eg_cols[ci] - aw_aus[ci][:, :HEAD_DIM]
            lhs_ref[slot, ci] = jnp.concatenate([q_eff, pns[ci][:, :HEAD_DIM]], axis=0).astype(BF16)
            add_ref[slot, ci] = jnp.concatenate([aw_aus[ci][:, HEAD_DIM:], pns[ci][:, HEAD_DIM:]], axis=0)

    def scan_chunk(slot, ci, state):
        both = _dot(lhs_ref[slot, ci], state.astype(BF16))
        o = both[:c] + add_ref[slot, ci, :c]
        gl = gl_ref[pl.ds((g - 1) * DN_GROUP + ci, 1), :]
        state = state * jnp.exp(gl) - both[c:] + add_ref[slot, ci, c:]
        o = o * lax.rsqrt(jnp.mean(o * o, axis=-1, keepdims=True) + RMS_EPS) * nw_ref[...]
        zr = z_ref[ci * c:(ci + 1) * c, :].astype(F32)
        o_ref[ci * c:(ci + 1) * c, :] = (o * (zr * _sigmoid(zr))).astype(o_ref.dtype)
        return state

    def run(first, with_scan):
        box = [state_ref[...]] if with_scan else None

        def scan_hook(ci):
            box[0] = scan_chunk(1 - (g & 1), ci, box[0])

        per_stage = [[] for _ in range(_DN_STAGES)]
        for i in range(GATE_K_PIECES):
            per_stage[i * (_DN_STAGES // GATE_K_PIECES)].append(functools.partial(gate_piece, i))
        if with_scan:
            for ci in range(DN_GROUP):
                per_stage[2 * ci + 1].append(functools.partial(scan_hook, ci))
        local_work(first, per_stage)
        if with_scan:
            state_ref[...] = box[0]

    @pl.when(g == 0)
    def _():
        run(True, False)

    @pl.when(jnp.logical_and(g > 0, g < n_groups))
    def _():
        run(False, True)

    @pl.when(g == n_groups)
    def _():
        state = state_ref[...]
        for ci in range(DN_GROUP):
            state = scan_chunk((n_groups - 1) & 1, ci, state)


def _deltanet_and_gates(h, conv_w, ba, a_log, dt_bias, norm_w, xb, w_gate, b_gate, batch, seq, heads, col0):
    c = DN_CHUNK
    rows = DN_GROUP * c
    assert seq % rows == 0 and w_gate.shape[0] == 3
    n_chunks, ng = seq // c, seq // rows
    m, d = xb.shape
    gw = d // heads
    assert gw * heads == d and gw % HEAD_DIM == 0 and d % GATE_K_PIECES == 0
    masks = jnp.asarray(_dn_masks())
    lane = lambda v: jnp.broadcast_to(v.astype(F32)[:, None, None], (heads, 1, c))
    body = functools.partial(_dn_body, n_groups=ng)
    cur = lambda b, g: b * ng + jnp.minimum(g, ng - 1)
    prev = lambda b, g: b * ng + jnp.maximum(g - 1, 0)
    hcol = lambda t: pl.BlockSpec((rows, HEAD_DIM), lambda b, h_, g: (cur(b, g), col0 + t * heads + h_))
    cw = lambda t: pl.BlockSpec((CONV_K, HEAD_DIM), lambda b, h_, g: (0, t * heads + h_))
    wgs = lambda t: pl.BlockSpec((None, d, gw), lambda b, h_, g: (t, 0, h_))
    bgs = lambda t: pl.BlockSpec((None, 1, gw), lambda b, h_, g: (t, 0, h_))
    b_gate = b_gate.reshape(3, 1, d)
    gate_out = pl.BlockSpec((rows, gw), lambda b, h_, g: (cur(b, g), h_))
    outs = pl.pallas_call(
        body,
        grid=(batch, heads, ng + 1),
        in_specs=[hcol(0), hcol(1), hcol(2),
                  pl.BlockSpec((rows, HEAD_DIM), lambda b, h_, g: (prev(b, g), col0 + 3 * heads + h_)),
                  cw(0), cw(1), cw(2),
                  pl.BlockSpec((1, 1, n_chunks, c), lambda b, h_, g: (b, h_, 0, 0)),
                  pl.BlockSpec((1, 1, n_chunks, c), lambda b, h_, g: (b, heads + h_, 0, 0)),
                  pl.BlockSpec((1, 1, c), lambda b, h_, g: (h_, 0, 0)),
                  pl.BlockSpec((1, 1, c), lambda b, h_, g: (h_, 0, 0)),
                  pl.BlockSpec((1, HEAD_DIM), lambda b, h_, g: (0, 0)),
                  pl.BlockSpec((_N_MASKS, c, c), lambda b, h_, g: (0, 0, 0)),
                  pl.BlockSpec((rows, d), lambda b, h_, g: (cur(b, g), 0)),
                  wgs(0), wgs(1), wgs(2), bgs(0), bgs(1), bgs(2)],
        out_specs=[pl.BlockSpec((rows, HEAD_DIM), lambda b, h_, g: (prev(b, g), h_)), gate_out, gate_out, gate_out],
        out_shape=[jax.ShapeDtypeStruct((m, heads * HEAD_DIM), BF16)] + [jax.ShapeDtypeStruct((m, d), BF16)] * 3,
        scratch_shapes=[pltpu.VMEM((n_chunks, c), F32)] * 3
        + [pltpu.VMEM((3, CONV_HALO + rows, HEAD_DIM), F32),
           pltpu.VMEM((2, DN_GROUP, 2 * c, HEAD_DIM), BF16), pltpu.VMEM((2, DN_GROUP, 2 * c, HEAD_DIM), F32),
           pltpu.VMEM((HEAD_DIM, HEAD_DIM), F32), pltpu.VMEM((3, rows, gw), F32)],
        compiler_params=_params(3),
        name="deltanet_and_gates",
    )(h, h, h, h, conv_w, conv_w, conv_w, ba, ba, lane(a_log), lane(dt_bias), norm_w.reshape(1, HEAD_DIM), masks,
      xb, w_gate, w_gate, w_gate, b_gate, b_gate, b_gate)
    return outs[0], outs[1:]


def _mem_body(q_ref, z_ref, kv_ref, o_ref, *, width):
    for hh in range(width // HEAD_DIM):
        sl = slice(hh * HEAD_DIM, (hh + 1) * HEAD_DIM)
        q = (q_ref[:, sl] * (HEAD_DIM ** -0.5)).astype(BF16)
        k = kv_ref[:, sl]
        v = kv_ref[:, width + hh * HEAD_DIM:width + (hh + 1) * HEAD_DIM]
        s = _dot_nt(q, k)
        p = jnp.exp(s - jnp.max(s, axis=-1, keepdims=True))
        o = _dot(p.astype(BF16), v) * (1.0 / jnp.sum(p, axis=-1, keepdims=True))
        z = z_ref[:, sl]
        o_ref[:, sl] = (o * (z * _sigmoid(z))).astype(o_ref.dtype)


def _mem_attention(h_tail, kv, batch, seq, n_mem, width, tq):
    nq = seq // tq
    body = functools.partial(_mem_body, width=width)
    return pl.pallas_call(
        body,
        grid=(batch, nq),
        in_specs=[pl.BlockSpec((tq, width), lambda b, i: (b * nq + i, 0)),
                  pl.BlockSpec((tq, width), lambda b, i: (b * nq + i, 1)),
                  pl.BlockSpec((n_mem, 2 * width), lambda b, i: (b, 0))],
        out_specs=pl.BlockSpec((tq, width), lambda b, i: (b * nq + i, 0)),
        out_shape=jax.ShapeDtypeStruct((batch * seq, width), BF16),
        compiler_params=_params(2),
        name="memory_attention",
    )(h_tail, h_tail, kv)


def _layer(x, mem, w_in, conv_w, a_log, dt_bias, dn_norm_w, w_mem_kv, w_gate, b_gate,
           w_up_sb, w_up_dn, w_up_mem, w_out, ln_g, ln_b, alpha):
    batch, seq, d = x.shape
    n_mem = mem.shape[1]
    m = batch * seq
    sbw, dnw, memw = w_up_sb.shape[0], w_up_dn.shape[0], w_up_mem.shape[0]
    sbh, dnh = sbw // HEAD_DIM, dnw // HEAD_DIM
    n_main = 4 * sbw + 4 * dnw
    ba0 = n_main
    qm0 = ba0 + 2 * dnh
    assert w_in.shape[1] == qm0 + 2 * memw

    x2 = x.reshape(m, d)
    tm = _tile(m, 1024)

    w_in_b = w_in.astype(BF16)
    pad = (-2 * dnh) % HEAD_DIM
    w_tail = jnp.concatenate([w_in_b[:, qm0:], w_in_b[:, ba0:qm0], jnp.zeros((d, pad), BF16)], axis=1)
    h_tail, xb = _tail_projection(x2, w_tail, _tile(m, 512))
    col_scale = jnp.concatenate([jnp.full((sbw,), HEAD_DIM ** -0.5 * LOG2E, F32), jnp.ones((n_main - sbw,), F32)])
    h = _tiled_matmul(_proj_scale_body, xb, w_in_b,
                      [(col_scale.reshape(1, n_main), (1, _tile(n_main, 1024)), lambda j, i: (0, j))],
                      BF16, tm, _tile(n_main, 1024), "input_projection", n_cols=n_main)

    o_sb = _sb_attention(h, batch, seq, sbh, 0)

    ba = h_tail[:, 2 * memw:2 * memw + 2 * dnh].reshape(batch, seq, 2 * dnh)
    ba = ba.transpose(0, 2, 1).reshape(batch, 2 * dnh, seq // DN_CHUNK, DN_CHUNK)
    o_dn, gates = _deltanet_and_gates(h, conv_w, ba, a_log, dt_bias, dn_norm_w, xb, w_gate.astype(BF16), b_gate,
                                      batch, seq, dnh, 4 * sbh)

    kv = _tiled_matmul(_proj_body, mem.reshape(batch * n_mem, d).astype(BF16), w_mem_kv.astype(BF16), [], BF16,
                       _tile(batch * n_mem, 512), _tile(2 * memw, 1024), "memory_kv_projection")
    o_m = _mem_attention(h_tail, kv, batch, seq, n_mem, memw, _tile(seq, 512))

    y = _merge(o_sb, o_dn, o_m, w_up_sb.astype(BF16), w_up_dn.astype(BF16), w_up_mem.astype(BF16), gates, tm,
               _tile(d, 512))
    tn_out = _tile(d, 512)
    r = _tiled_matmul(functools.partial(_resid_body, alpha=alpha), y, w_out.astype(BF16),
                      [(x2, (tm, tn_out), lambda j, i: (i, j))], F32, tm, tn_out, "output_projection")
    return _layernorm(r, ln_g, ln_b, _tile(m, 256)).reshape(batch, seq, d)


def kernel(x, mem, w_in, conv_w, a_log, dt_bias, dn_norm_w, w_mem_kv, w_gate, b_gate, w_up_sb, w_up_dn, w_up_mem,
           w_out, ln_g, ln_b):
    depth = w_in.shape[0]
    alpha = (2.0 * depth) ** 0.25
    for l in range(depth):
        x = _layer(x, mem, w_in[l], conv_w[l], a_log[l], dt_bias[l], dn_norm_w[l], w_mem_kv[l], w_gate[l],
                   b_gate[l], w_up_sb[l], w_up_dn[l], w_up_mem[l], w_out[l], ln_g[l], ln_b[l], alpha)
    return x
```

```python
import functools

import numpy as np
import jax
import jax.numpy as jnp
from jax import lax
from jax.experimental import pallas as pl
from jax.experimental.pallas import tpu as pltpu

HEAD_DIM = 128
CONV_K = 4
CONV_HALO = 8
MEM_HEADS = 4
LN_EPS = 1e-5
RMS_EPS = 1e-6
L2_EPS = 1e-6
LOG2E = 1.4426950408889634
SOFTPLUS2_CLAMP = 64.0
MASKED_LOG2_WEIGHT = -1e30
SB_TRI_EXTRA_ROWS = 16
SB_UNDERFLOW_LOG2 = 1100.0

DN_CHUNK = 128
DN_GROUP = 4
SB_BLOCK = 256
SB_HEADS_PER_STEP = 4
V7X_VMEM_LIMIT_BYTES = 48 * 1024 * 1024

F32 = jnp.float32
BF16 = jnp.bfloat16


def _params(n_axes):
    return pltpu.CompilerParams(dimension_semantics=("arbitrary",) * n_axes,
                                vmem_limit_bytes=V7X_VMEM_LIMIT_BYTES)


def _dot(a, b):
    return jnp.dot(a, b, preferred_element_type=F32)


def _dot_nt(a, b):
    return lax.dot_general(a, b, (((1,), (1,)), ((), ())), preferred_element_type=F32)


def _sigmoid(x):
    return 1.0 / (1.0 + jnp.exp(-x))


def _softplus(x):
    return jnp.maximum(x, 0.0) + jnp.log(1.0 + jnp.exp(-jnp.abs(x)))


def _softplus2(x):
    return jnp.maximum(x, jnp.log(1.0 + jnp.exp2(jnp.minimum(x, SOFTPLUS2_CLAMP))) * LOG2E)


def _tile(n, pref):
    t = min(n, pref)
    assert n % t == 0, (n, t)
    return t


def _tiled_matmul(body, a, b, extra, out_dtype, tm, tn, name, n_cols=None):
    m, k = a.shape
    if b.ndim == 3:
        nb = b.shape[2] // tn
        n = b.shape[0] * b.shape[2]
        b_spec = pl.BlockSpec((None, k, tn), lambda j, i: (j // nb, 0, j % nb))
    else:
        n = b.shape[1] if n_cols is None else n_cols
        b_spec = pl.BlockSpec((k, tn), lambda j, i: (0, j))
    assert m % tm == 0 and n % tn == 0
    grid = (n // tn, m // tm)
    in_specs = [pl.BlockSpec((tm, k), lambda j, i: (i, 0)), b_spec]
    in_specs += [pl.BlockSpec(blk, imap) for _, blk, imap in extra]
    return pl.pallas_call(
        body,
        grid=grid,
        in_specs=in_specs,
        out_specs=pl.BlockSpec((tm, tn), lambda j, i: (i, j)),
        out_shape=jax.ShapeDtypeStruct((m, n), out_dtype),
        compiler_params=_params(2),
        name=name,
    )(a, b, *[e[0] for e in extra])


def _proj_scale_body(a_ref, b_ref, s_ref, o_ref):
    o_ref[...] = (_dot(a_ref[...], b_ref[...]) * s_ref[...]).astype(o_ref.dtype)


def _tail_body(x_ref, w_ref, o_ref, xb_ref):
    xb = x_ref[...].astype(BF16)
    xb_ref[...] = xb
    o_ref[...] = _dot(xb, w_ref[...])


def _tail_projection(x2, w_tail, tm):
    m, d = x2.shape
    n = w_tail.shape[1]
    return pl.pallas_call(
        _tail_body,
        grid=(m // tm,),
        in_specs=[pl.BlockSpec((tm, d), lambda i: (i, 0)),
                  pl.BlockSpec((d, n), lambda i: (0, 0), pipeline_mode=pl.Buffered(1))],
        out_specs=[pl.BlockSpec((tm, n), lambda i: (i, 0)), pl.BlockSpec((tm, d), lambda i: (i, 0))],
        out_shape=[jax.ShapeDtypeStruct((m, n), F32), jax.ShapeDtypeStruct((m, d), BF16)],
        compiler_params=_params(1),
        name="input_projection_tail",
    )(x2, w_tail)


def _proj_body(a_ref, b_ref, o_ref):
    o_ref[...] = _dot(a_ref[...], b_ref[...]).astype(o_ref.dtype)


def _resid_body(a_ref, b_ref, x_ref, o_ref, *, alpha):
    o_ref[...] = alpha * x_ref[...] + _dot(a_ref[...], b_ref[...])


def _merge_body(osb_ref, odn_ref, om_ref, wsb_ref, wdn_ref, wm_ref, g0_ref, g1_ref, g2_ref, o_ref):
    y = g0_ref[...].astype(F32) * _dot(osb_ref[...], wsb_ref[...])
    y += g1_ref[...].astype(F32) * _dot(odn_ref[...], wdn_ref[...])
    y += g2_ref[...].astype(F32) * _dot(om_ref[...], wm_ref[...])
    o_ref[...] = y.astype(o_ref.dtype)


def _merge(o_sb, o_dn, o_m, w_sb, w_dn, w_m, gates, tm, tn):
    m = o_sb.shape[0]
    d = w_sb.shape[1]
    row = lambda w: pl.BlockSpec((tm, w), lambda j, i: (i, 0))
    colw = lambda w: pl.BlockSpec((w, tn), lambda j, i: (0, j))
    gate = pl.BlockSpec((tm, tn), lambda j, i: (i, j))
    return pl.pallas_call(
        _merge_body,
        grid=(d // tn, m // tm),
        in_specs=[row(o_sb.shape[1]), row(o_dn.shape[1]), row(o_m.shape[1]),
                  colw(w_sb.shape[0]), colw(w_dn.shape[0]), colw(w_m.shape[0]),
                  gate, gate, gate],
        out_specs=pl.BlockSpec((tm, tn), lambda j, i: (i, j)),
        out_shape=jax.ShapeDtypeStruct((m, d), BF16),
        compiler_params=_params(2),
        name="branch_merge",
    )(o_sb, o_dn, o_m, w_sb, w_dn, w_m, *gates)


def _layernorm_body(r_ref, g_ref, b_ref, o_ref):
    r = r_ref[...]
    mu = jnp.mean(r, axis=-1, keepdims=True)
    rc = r - mu
    var = jnp.mean(rc * rc, axis=-1, keepdims=True)
    o_ref[...] = rc * lax.rsqrt(var + LN_EPS) * g_ref[...] + b_ref[...]


def _layernorm(r, g, b, tm):
    m, d = r.shape
    return pl.pallas_call(
        _layernorm_body,
        grid=(m // tm,),
        in_specs=[pl.BlockSpec((tm, d), lambda i: (i, 0)),
                  pl.BlockSpec((1, d), lambda i: (0, 0)),
                  pl.BlockSpec((1, d), lambda i: (0, 0))],
        out_specs=pl.BlockSpec((tm, d), lambda i: (i, 0)),
        out_shape=jax.ShapeDtypeStruct((m, d), F32),
        compiler_params=_params(1),
        name="post_layernorm",
    )(r, g.reshape(1, d), b.reshape(1, d))


def _sb_body(q_ref, k_ref, v_ref, z_ref, tri_ref, o_ref, vt_ref, acc_ref, lsig_ref, sp_ref, *, tk, nq, hps):
    qi = pl.program_id(2)
    heads = range(hps)
    lanes = lambda hh: slice(hh * HEAD_DIM, (hh + 1) * HEAD_DIM)

    @pl.when(qi == 0)
    def _():
        def transpose_block(c, _):
            blk = v_ref[pl.ds(pl.multiple_of(c * tk, tk), tk), :].astype(F32)
            for hh in heads:
                vt_ref[hh, c] = blk[:, lanes(hh)].T.astype(BF16)
            return 0
        lax.fori_loop(0, nq, transpose_block, 0)

    qs = [q_ref[:, lanes(hh)] for hh in heads]
    key_pos = lax.broadcasted_iota(jnp.int32, (tk, tk), 0)
    qry_pos = lax.broadcasted_iota(jnp.int32, (tk, tk), 1)
    before = key_pos < qry_pos

    def score_dots(j):
        kb = k_ref[pl.ds(pl.multiple_of(j * tk, tk), tk), :]
        return [_dot_nt(kb[:, lanes(hh)], qs[hh]) for hh in heads]

    def scores_store(zts, slot, diagonal):
        for hh in heads:
            sp = _softplus2(zts[hh])
            lsig = zts[hh] - sp
            if diagonal:
                sp = jnp.where(before, sp, 0.0)
                lsig = jnp.where(before, lsig, MASKED_LOG2_WEIGHT)
            lsig_ref[slot, hh] = lsig
            sp_ref[slot, hh] = sp.astype(BF16)

    def suffix_dots(slot):
        return [_dot(tri_ref[...], sp_ref[slot, hh]) for hh in heads]

    def finish(j, slot, sufs, carries):
        out = []
        for hh in heads:
            w = jnp.exp2(lsig_ref[slot, hh] - sufs[hh][:tk] - carries[hh])
            acc_ref[hh] += _dot(vt_ref[hh, j], w.astype(BF16))
            out.append(carries[hh] + sufs[hh][tk:tk + 1])
        return tuple(out)

    def step(state):
        t, carries, _ = state
        slot = t & 1
        sufs = suffix_dots(slot)
        zts = score_dots(qi - t - 1)
        carries = finish(qi - t, slot, sufs, carries)
        scores_store(zts, 1 - slot, False)
        smallest = functools.reduce(jnp.minimum, carries)
        return t + 1, carries, jnp.min(smallest) < SB_UNDERFLOW_LOG2

    acc_ref[...] = jnp.zeros(acc_ref.shape, F32)
    scores_store(score_dots(qi), 0, True)
    t, carries, _ = lax.while_loop(lambda s: jnp.logical_and(s[0] < qi, s[2]), step,
                                   (jnp.int32(0), (jnp.zeros((1, tk), F32),) * hps, jnp.bool_(True)))
    last = t & 1
    finish(qi - t, last, suffix_dots(last), carries)

    for hh in heads:
        z = z_ref[:, lanes(hh)].astype(F32)
        o_ref[:, lanes(hh)] = (acc_ref[hh].T * (z * _sigmoid(z))).astype(o_ref.dtype)


def _sb_attention(h, batch, seq, heads, col0):
    t = _tile(seq, SB_BLOCK)
    nq = seq // t
    hps = SB_HEADS_PER_STEP if heads % SB_HEADS_PER_STEP == 0 and col0 % SB_HEADS_PER_STEP == 0 else 1
    ng = heads // hps
    c0 = col0 // hps
    w = hps * HEAD_DIM
    tri = np.zeros((t + SB_TRI_EXTRA_ROWS, t), np.float32)
    tri[:t] = np.triu(np.ones((t, t), np.float32), 1)
    tri[t] = 1.0
    tri = jnp.asarray(tri, BF16)
    body = functools.partial(_sb_body, tk=t, nq=nq, hps=hps)
    whole_seq = lambda g: pl.BlockSpec((seq, w), lambda b, h_, i: (b, c0 + g * ng + h_), pipeline_mode=pl.Buffered(1))
    return pl.pallas_call(
        body,
        grid=(batch, ng, nq),
        in_specs=[pl.BlockSpec((t, w), lambda b, h_, i: (b * nq + i, c0 + h_)),
                  whole_seq(1), whole_seq(2),
                  pl.BlockSpec((t, w), lambda b, h_, i: (b * nq + i, c0 + 3 * ng + h_)),
                  pl.BlockSpec((t + SB_TRI_EXTRA_ROWS, t), lambda b, h_, i: (0, 0))],
        out_specs=pl.BlockSpec((t, w), lambda b, h_, i: (b * nq + i, h_)),
        out_shape=jax.ShapeDtypeStruct((batch * seq, heads * HEAD_DIM), BF16),
        scratch_shapes=[pltpu.VMEM((hps, nq, HEAD_DIM, t), BF16), pltpu.VMEM((hps, HEAD_DIM, t), F32),
                        pltpu.VMEM((2, hps, t, t), F32), pltpu.VMEM((2, hps, t, t), BF16)],
        compiler_params=_params(3),
        name="stickbreak_attention",
    )(h, h, h, h, tri)


_M_INCL, _M_STRICT, _M_PAIR, _M_LEVEL0 = 0, 1, 2, 3
_N_LEVELS = int(np.log2(DN_CHUNK)) - 1
_M_EYE = _M_LEVEL0 + _N_LEVELS
_M_UPPER = _M_EYE + 1
_N_MASKS = _M_UPPER + 1
_DN_STAGES = 2 * _N_LEVELS + 4
GATE_K_PIECES = 4
assert _DN_STAGES % GATE_K_PIECES == 0


def _dn_masks():
    c = DN_CHUNK
    i = np.arange(c)[:, None]
    j = np.arange(c)[None, :]
    masks = [i >= j, i > j, (i % 2 == 1) & (j == i - 1)]
    b = 4
    while b <= c:
        masks.append((i // b == j // b) & (i % b >= b // 2) & (j % b < b // 2))
        b *= 2
    masks += [i == j, i <= j]
    assert len(masks) == _N_MASKS
    return np.stack(masks).astype(np.float32)


def _split3(x):
    p1 = x.astype(BF16)
    r = x - p1.astype(F32)
    p2 = r.astype(BF16)
    p3 = (r - p2.astype(F32)).astype(BF16)
    return p1, p2, p3


def _dn_body(q_ref, k_ref, v_ref, z_ref, cq_ref, ck_ref, cv_ref, b_ref, a_ref, alog_ref, dtb_ref, nw_ref, msk_ref,
             x_ref, wg0_ref, wg1_ref, wg2_ref, bg0_ref, bg1_ref, bg2_ref,
             o_ref, g0_ref, g1_ref, g2_ref,
             gc_ref, beta_ref, gl_ref, xf_ref, lhs_ref, add_ref, state_ref, acc_ref, *, n_groups):
    c = DN_CHUNK
    rows = DN_GROUP * c
    g = pl.program_id(2)
    wg_refs, bg_refs, gate_refs = (wg0_ref, wg1_ref, wg2_ref), (bg0_ref, bg1_ref, bg2_ref), (g0_ref, g1_ref, g2_ref)
    k_piece = x_ref.shape[1] // GATE_K_PIECES

    @pl.when(g == 0)
    def _():
        gd = -jnp.exp(alog_ref[0]) * _softplus(a_ref[0, 0] + dtb_ref[0])
        upper = msk_ref[_M_UPPER].astype(BF16)
        ones = jnp.ones((c, c), BF16)
        pieces = _split3(gd)
        gc_ref[...] = sum(_dot(p, upper) for p in pieces)
        gl_ref[...] = sum(_dot(p, ones) for p in pieces)
        beta_ref[...] = _sigmoid(b_ref[0, 0])
        state_ref[...] = jnp.zeros(state_ref.shape, F32)

    def gate_piece(i):
        ks = slice(i * k_piece, (i + 1) * k_piece)
        for t in range(3):
            part = _dot(x_ref[:, ks], wg_refs[t][ks, :])
            if i == 0:
                acc_ref[t] = part
            elif i < GATE_K_PIECES - 1:
                acc_ref[t] += part
            else:
                gate_refs[t][...] = _sigmoid(acc_ref[t] + part + bg_refs[t][...]).astype(gate_refs[t].dtype)

    def conv_silu(idx, x_blk_ref, cw_ref, first):
        if first:
            xf_ref[idx, 0:CONV_HALO, :] = jnp.zeros((CONV_HALO, HEAD_DIM), F32)
        else:
            xf_ref[idx, 0:CONV_HALO, :] = xf_ref[idx, rows:rows + CONV_HALO, :]
        xf_ref[idx, CONV_HALO:, :] = x_blk_ref[...].astype(F32)
        cw = cw_ref[...]
        y = None
        for s in range(CONV_K):
            tap = xf_ref[idx, CONV_HALO - s:CONV_HALO - s + rows, :] * cw[CONV_K - 1 - s:CONV_K - s, :]
            y = tap if y is None else y + tap
        return y * _sigmoid(y)

    def l2norm(t):
        return t * lax.rsqrt(jnp.sum(t * t, axis=-1, keepdims=True) + L2_EPS)

    def local_work(first, per_stage):
        stage = [0]

        def stage_done():
            for fn in per_stage[stage[0]]:
                fn()
            stage[0] += 1

        slot = g & 1
        qg = l2norm(conv_silu(0, q_ref, cq_ref, first)) * (HEAD_DIM ** -0.5)
        kg = l2norm(conv_silu(1, k_ref, ck_ref, first))
        vg = conv_silu(2, v_ref, cv_ref, first)
        cs = range(DN_GROUP)
        sl = lambda ci: slice(ci * c, (ci + 1) * c)
        qs, ks, vs = [qg[sl(ci)] for ci in cs], [kg[sl(ci)] for ci in cs], [vg[sl(ci)] for ci in cs]
        gls, gc_cols, beta_cols, decays = [], [], [], []
        for ci in cs:
            chunk = g * DN_GROUP + ci
            gc_row = jnp.broadcast_to(gc_ref[pl.ds(chunk, 1), :], (c, c))
            gc_col = gc_row.T
            incl = msk_ref[_M_INCL]
            gls.append(gl_ref[pl.ds(chunk, 1), :])
            gc_cols.append(gc_col)
            beta_cols.append(jnp.broadcast_to(beta_ref[pl.ds(chunk, 1), :], (c, c)).T)
            decays.append(jnp.exp((gc_col - gc_row) * incl) * incl)
        kbs = [ks[ci] * beta_cols[ci] for ci in cs]
        prods = [_dot_nt(jnp.concatenate([qs[ci], kbs[ci]], axis=0).astype(BF16), ks[ci].astype(BF16)) for ci in cs]
        stage_done()
        a_ins = [prods[ci][:c] * decays[ci] for ci in cs]
        a_los = [prods[ci][c:] * decays[ci] * msk_ref[_M_STRICT] for ci in cs]
        t_invs = [msk_ref[_M_EYE] - a_lo * msk_ref[_M_PAIR] for a_lo in a_los]
        for lvl in range(_N_LEVELS):
            offs = [(a_lo * msk_ref[_M_LEVEL0 + lvl]).astype(BF16) for a_lo in a_los]
            tbs = [t_inv.astype(BF16) for t_inv in t_invs]
            xs = [_dot(tbs[ci], offs[ci]).astype(BF16) for ci in cs]
            stage_done()
            t_invs = [t_invs[ci] - _dot(xs[ci], tbs[ci]) for ci in cs]
            stage_done()
        eg_cols = [jnp.exp(gc_col) for gc_col in gc_cols]
        rhss = [jnp.concatenate([kbs[ci] * eg_cols[ci], vs[ci] * beta_cols[ci]], axis=1).astype(BF16) for ci in cs]
        wus = [_dot(t_invs[ci].astype(BF16), rhss[ci]).astype(BF16) for ci in cs]
        stage_done()
        k_ends = [ks[ci] * jnp.exp(gls[ci] - gc_cols[ci]) for ci in cs]
        pns = [_dot(k_ends[ci].T.astype(BF16), wus[ci]) for ci in cs]
        stage_done()
        aw_aus = [_dot(a_ins[ci].astype(BF16), wus[ci]) for ci in cs]
        stage_done()
        assert stage[0] == _DN_STAGES == len(per_stage)
        for ci in cs:
            q_eff = qs[ci] * eg_cols[ci] - aw_aus[ci][:, :HEAD_DIM]
            lhs_ref[slot, ci] = jnp.concatenate([q_eff, pns[ci][:, :HEAD_DIM]], axis=0).astype(BF16)
            add_ref[slot, ci] = jnp.concatenate([aw_aus[ci][:, HEAD_DIM:], pns[ci][:, HEAD_DIM:]], axis=0)

    def scan_chunk(slot, ci, state):
        both = _dot(lhs_ref[slot, ci], state.astype(BF16))
        o = both[:c] + add_ref[slot, ci, :c]
        gl = gl_ref[pl.ds((g - 1) * DN_GROUP + ci, 1), :]
        state = state * jnp.exp(gl) - both[c:] + add_ref[slot, ci, c:]
        o = o * lax.rsqrt(jnp.mean(o * o, axis=-1, keepdims=True) + RMS_EPS) * nw_ref[...]
        zr = z_ref[ci * c:(ci + 1) * c, :].astype(F32)
        o_ref[ci * c:(ci + 1) * c, :] = (o * (zr * _sigmoid(zr))).astype(o_ref.dtype)
        return state

    def run(first, with_scan):
        box = [state_ref[...]] if with_scan else None

        def scan_hook(ci):
            box[0] = scan_chunk(1 - (g & 1), ci, box[0])

        per_stage = [[] for _ in range(_DN_STAGES)]
        for i in range(GATE_K_PIECES):
            per_stage[i * (_DN_STAGES // GATE_K_PIECES)].append(functools.partial(gate_piece, i))
        if with_scan:
            for ci in range(DN_GROUP):
                per_stage[2 * ci + 1].append(functools.partial(scan_hook, ci))
        local_work(first, per_stage)
        if with_scan:
            state_ref[...] = box[0]

    @pl.when(g == 0)
    def _():
        run(True, False)

    @pl.when(jnp.logical_and(g > 0, g < n_groups))
    def _():
        run(False, True)

    @pl.when(g == n_groups)
    def _():
        state = state_ref[...]
        for ci in range(DN_GROUP):
            state = scan_chunk((n_groups - 1) & 1, ci, state)


def _deltanet_and_gates(h, conv_w, ba, a_log, dt_bias, norm_w, xb, w_gate, b_gate, batch, seq, heads, col0):
    c = DN_CHUNK
    rows = DN_GROUP * c
    assert seq % rows == 0 and w_gate.shape[0] == 3
    n_chunks, ng = seq // c, seq // rows
    m, d = xb.shape
    gw = d // heads
    assert gw * heads == d and gw % HEAD_DIM == 0 and d % GATE_K_PIECES == 0
    masks = jnp.asarray(_dn_masks())
    lane = lambda v: jnp.broadcast_to(v.astype(F32)[:, None, None], (heads, 1, c))
    body = functools.partial(_dn_body, n_groups=ng)
    cur = lambda b, g: b * ng + jnp.minimum(g, ng - 1)
    prev = lambda b, g: b * ng + jnp.maximum(g - 1, 0)
    hcol = lambda t: pl.BlockSpec((rows, HEAD_DIM), lambda b, h_, g: (cur(b, g), col0 + t * heads + h_))
    cw = lambda t: pl.BlockSpec((CONV_K, HEAD_DIM), lambda b, h_, g: (0, t * heads + h_))
    wgs = lambda t: pl.BlockSpec((None, d, gw), lambda b, h_, g: (t, 0, h_))
    bgs = lambda t: pl.BlockSpec((None, 1, gw), lambda b, h_, g: (t, 0, h_))
    b_gate = b_gate.reshape(3, 1, d)
    gate_out = pl.BlockSpec((rows, gw), lambda b, h_, g: (cur(b, g), h_))
    outs = pl.pallas_call(
        body,
        grid=(batch, heads, ng + 1),
        in_specs=[hcol(0), hcol(1), hcol(2),
                  pl.BlockSpec((rows, HEAD_DIM), lambda b, h_, g: (prev(b, g), col0 + 3 * heads + h_)),
                  cw(0), cw(1), cw(2),
                  pl.BlockSpec((1, 1, n_chunks, c), lambda b, h_, g: (b, h_, 0, 0)),
                  pl.BlockSpec((1, 1, n_chunks, c), lambda b, h_, g: (b, heads + h_, 0, 0)),
                  pl.BlockSpec((1, 1, c), lambda b, h_, g: (h_, 0, 0)),
                  pl.BlockSpec((1, 1, c), lambda b, h_, g: (h_, 0, 0)),
                  pl.BlockSpec((1, HEAD_DIM), lambda b, h_, g: (0, 0)),
                  pl.BlockSpec((_N_MASKS, c, c), lambda b, h_, g: (0, 0, 0)),
                  pl.BlockSpec((rows, d), lambda b, h_, g: (cur(b, g), 0)),
                  wgs(0), wgs(1), wgs(2), bgs(0), bgs(1), bgs(2)],
        out_specs=[pl.BlockSpec((rows, HEAD_DIM), lambda b, h_, g: (prev(b, g), h_)), gate_out, gate_out, gate_out],
        out_shape=[jax.ShapeDtypeStruct((m, heads * HEAD_DIM), BF16)] + [jax.ShapeDtypeStruct((m, d), BF16)] * 3,
        scratch_shapes=[pltpu.VMEM((n_chunks, c), F32)] * 3
        + [pltpu.VMEM((3, CONV_HALO + rows, HEAD_DIM), F32),
           pltpu.VMEM((2, DN_GROUP, 2 * c, HEAD_DIM), BF16), pltpu.VMEM((2, DN_GROUP, 2 * c, HEAD_DIM), F32),
           pltpu.VMEM((HEAD_DIM, HEAD_DIM), F32), pltpu.VMEM((3, rows, gw), F32)],
        compiler_params=_params(3),
        name="deltanet_and_gates",
    )(h, h, h, h, conv_w, conv_w, conv_w, ba, ba, lane(a_log), lane(dt_bias), norm_w.reshape(1, HEAD_DIM), masks,
      xb, w_gate, w_gate, w_gate, b_gate, b_gate, b_gate)
    return outs[0], outs[1:]


def _mem_body(q_ref, z_ref, kv_ref, o_ref, *, width):
    for hh in range(width // HEAD_DIM):
        sl = slice(hh * HEAD_DIM, (hh + 1) * HEAD_DIM)
        q = (q_ref[:, sl] * (HEAD_DIM ** -0.5)).astype(BF16)
        k = kv_ref[:, sl]
        v = kv_ref[:, width + hh * HEAD_DIM:width + (hh + 1) * HEAD_DIM]
        s = _dot_nt(q, k)
        p = jnp.exp(s - jnp.max(s, axis=-1, keepdims=True))
        o = _dot(p.astype(BF16), v) * (1.0 / jnp.sum(p, axis=-1, keepdims=True))
        z = z_ref[:, sl]
        o_ref[:, sl] = (o * (z * _sigmoid(z))).astype(o_ref.dtype)


def _mem_attention(h_tail, kv, batch, seq, n_mem, width, tq):
    nq = seq // tq
    body = functools.partial(_mem_body, width=width)
    return pl.pallas_call(
        body,
        grid=(batch, nq),
        in_specs=[pl.BlockSpec((tq, width), lambda b, i: (b * nq + i, 0)),
                  pl.BlockSpec((tq, width), lambda b, i: (b * nq + i, 1)),
                  pl.BlockSpec((n_mem, 2 * width), lambda b, i: (b, 0))],
        out_specs=pl.BlockSpec((tq, width), lambda b, i: (b * nq + i, 0)),
        out_shape=jax.ShapeDtypeStruct((batch * seq, width), BF16),
        compiler_params=_params(2),
        name="memory_attention",
    )(h_tail, h_tail, kv)


def _layer(x, mem, w_in, conv_w, a_log, dt_bias, dn_norm_w, w_mem_kv, w_gate, b_gate,
           w_up_sb, w_up_dn, w_up_mem, w_out, ln_g, ln_b, alpha):
    batch, seq, d = x.shape
    n_mem = mem.shape[1]
    m = batch * seq
    sbw, dnw, memw = w_up_sb.shape[0], w_up_dn.shape[0], w_up_mem.shape[0]
    sbh, dnh = sbw // HEAD_DIM, dnw // HEAD_DIM
    n_main = 4 * sbw + 4 * dnw
    ba0 = n_main
    qm0 = ba0 + 2 * dnh
    assert w_in.shape[1] == qm0 + 2 * memw

    x2 = x.reshape(m, d)
    tm = _tile(m, 1024)

    w_in_b = w_in.astype(BF16)
    pad = (-2 * dnh) % HEAD_DIM
    w_tail = jnp.concatenate([w_in_b[:, qm0:], w_in_b[:, ba0:qm0], jnp.zeros((d, pad), BF16)], axis=1)
    h_tail, xb = _tail_projection(x2, w_tail, _tile(m, 512))
    col_scale = jnp.concatenate([jnp.full((sbw,), HEAD_DIM ** -0.5 * LOG2E, F32), jnp.ones((n_main - sbw,), F32)])
    h = _tiled_matmul(_proj_scale_body, xb, w_in_b,
                      [(col_scale.reshape(1, n_main), (1, _tile(n_main, 1024)), lambda j, i: (0, j))],
                      BF16, tm, _tile(n_main, 1024), "input_projection", n_cols=n_main)

    o_sb = _sb_attention(h, batch, seq, sbh, 0)

    ba = h_tail[:, 2 * memw:2 * memw + 2 * dnh].reshape(batch, seq, 2 * dnh)
    ba = ba.transpose(0, 2, 1).reshape(batch, 2 * dnh, seq // DN_CHUNK, DN_CHUNK)
    o_dn, gates = _deltanet_and_gates(h, conv_w, ba, a_log, dt_bias, dn_norm_w, xb, w_gate.astype(BF16), b_gate,
                                      batch, seq, dnh, 4 * sbh)

    kv = _tiled_matmul(_proj_body, mem.reshape(batch * n_mem, d).astype(BF16), w_mem_kv.astype(BF16), [], BF16,
                       _tile(batch * n_mem, 512), _tile(2 * memw, 1024), "memory_kv_projection")
    o_m = _mem_attention(h_tail, kv, batch, seq, n_mem, memw, _tile(seq, 512))

    y = _merge(o_sb, o_dn, o_m, w_up_sb.astype(BF16), w_up_dn.astype(BF16), w_up_mem.astype(BF16), gates, tm,
               _tile(d, 512))
    tn_out = _tile(d, 512)
    r = _tiled_matmul(functools.partial(_resid_body, alpha=alpha), y, w_out.astype(BF16),
                      [(x2, (tm, tn_out), lambda j, i: (i, j))], F32, tm, tn_out, "output_projection")
    return _layernorm(r, ln_g, ln_b, _tile(m, 256)).reshape(batch, seq, d)


def kernel(x, mem, w_in, conv_w, a_log, dt_bias, dn_norm_w, w_mem_kv, w_gate, b_gate, w_up_sb, w_up_dn, w_up_mem,
           w_out, ln_g, ln_b):
    depth = w_in.shape[0]
    alpha = (2.0 * depth) ** 0.25
    for l in range(depth):
        x = _layer(x, mem, w_in[l], conv_w[l], a_log[l], dt_bias[l], dn_norm_w[l], w_mem_kv[l], w_gate[l],
                   b_gate[l], w_up_sb[l], w_up_dn[l], w_up_mem[l], w_out[l], ln_g[l], ln_b[l], alpha)
    return x
```

```python
import functools

import numpy as np
import jax
import jax.numpy as jnp
from jax import lax
from jax.experimental import pallas as pl
from jax.experimental.pallas import tpu as pltpu

HEAD_DIM = 128
CONV_K = 4
CONV_HALO = 8
LN_EPS = 1e-5
RMS_EPS = 1e-6
L2_EPS = 1e-6
LOG2E = 1.4426950408889634
SOFTPLUS2_CLAMP = 64.0
MASKED_LOG2_WEIGHT = -1e30
SB_TRI_EXTRA_ROWS = 16
SB_UNDERFLOW_LOG2 = 1100.0

DN_CHUNK = 128
DN_GROUP = 8
SB_BLOCK = 256
SB_HEADS_PER_STEP = 4
V7X_VMEM_LIMIT_BYTES = 48 * 1024 * 1024

F32 = jnp.float32
BF16 = jnp.bfloat16


def _params(n_axes):
    return pltpu.CompilerParams(dimension_semantics=("arbitrary",) * n_axes,
                                vmem_limit_bytes=V7X_VMEM_LIMIT_BYTES)


def _dot(a, b):
    return jnp.dot(a, b, preferred_element_type=F32)


def _dot_nt(a, b):
    return lax.dot_general(a, b, (((1,), (1,)), ((), ())), preferred_element_type=F32)


def _sigmoid(x):
    return 1.0 / (1.0 + jnp.exp(-x))


def _softplus(x):
    return jnp.maximum(x, 0.0) + jnp.log(1.0 + jnp.exp(-jnp.abs(x)))


def _softplus2(x):
    return jnp.maximum(x, jnp.log(1.0 + jnp.exp2(jnp.minimum(x, SOFTPLUS2_CLAMP))) * LOG2E)


def _tile(n, pref):
    t = min(n, pref)
    assert n % t == 0, (n, t)
    return t


def _tiled_matmul(body, a, b, extra, out_dtype, tm, tn, name, n_cols=None):
    m, k = a.shape
    if b.ndim == 3:
        nb = b.shape[2] // tn
        n = b.shape[0] * b.shape[2]
        b_spec = pl.BlockSpec((None, k, tn), lambda j, i: (j // nb, 0, j % nb))
    else:
        n = b.shape[1] if n_cols is None else n_cols
        b_spec = pl.BlockSpec((k, tn), lambda j, i: (0, j))
    assert m % tm == 0 and n % tn == 0
    grid = (n // tn, m // tm)
    in_specs = [pl.BlockSpec((tm, k), lambda j, i: (i, 0)), b_spec]
    in_specs += [pl.BlockSpec(blk, imap) for _, blk, imap in extra]
    return pl.pallas_call(
        body,
        grid=grid,
        in_specs=in_specs,
        out_specs=pl.BlockSpec((tm, tn), lambda j, i: (i, j)),
        out_shape=jax.ShapeDtypeStruct((m, n), out_dtype),
        compiler_params=_params(2),
        name=name,
    )(a, b, *[e[0] for e in extra])


def _proj_scale_body(a_ref, b_ref, s_ref, o_ref):
    o_ref[...] = (_dot(a_ref[...], b_ref[...]) * s_ref[...]).astype(o_ref.dtype)


def _tail_body(x_ref, w_ref, o_ref, xb_ref):
    xb = x_ref[...].astype(BF16)
    xb_ref[...] = xb
    o_ref[...] = _dot(xb, w_ref[...])


def _tail_projection(x2, w_tail, tm):
    m, d = x2.shape
    n = w_tail.shape[1]
    return pl.pallas_call(
        _tail_body,
        grid=(m // tm,),
        in_specs=[pl.BlockSpec((tm, d), lambda i: (i, 0)),
                  pl.BlockSpec((d, n), lambda i: (0, 0), pipeline_mode=pl.Buffered(1))],
        out_specs=[pl.BlockSpec((tm, n), lambda i: (i, 0)), pl.BlockSpec((tm, d), lambda i: (i, 0))],
        out_shape=[jax.ShapeDtypeStruct((m, n), F32), jax.ShapeDtypeStruct((m, d), BF16)],
        compiler_params=_params(1),
        name="input_projection_tail",
    )(x2, w_tail)


def _proj_body(a_ref, b_ref, o_ref):
    o_ref[...] = _dot(a_ref[...], b_ref[...]).astype(o_ref.dtype)


def _resid_body(a_ref, b_ref, x_ref, o_ref, *, alpha):
    o_ref[...] = alpha * x_ref[...] + _dot(a_ref[...], b_ref[...])


def _merge_body(osb_ref, odn_ref, om_ref, wsb_ref, wdn_ref, wm_ref, g0_ref, g1_ref, g2_ref, o_ref):
    y = g0_ref[...].astype(F32) * _dot(osb_ref[...], wsb_ref[...])
    y += g1_ref[...].astype(F32) * _dot(odn_ref[...], wdn_ref[...])
    y += g2_ref[...].astype(F32) * _dot(om_ref[...], wm_ref[...])
    o_ref[...] = y.astype(o_ref.dtype)


def _merge(o_sb, o_dn, o_m, w_sb, w_dn, w_m, gates, tm, tn):
    m = o_sb.shape[0]
    d = w_sb.shape[1]
    row = lambda w: pl.BlockSpec((tm, w), lambda j, i: (i, 0))
    colw = lambda w: pl.BlockSpec((w, tn), lambda j, i: (0, j))
    gate = pl.BlockSpec((tm, tn), lambda j, i: (i, j))
    return pl.pallas_call(
        _merge_body,
        grid=(d // tn, m // tm),
        in_specs=[row(o_sb.shape[1]), row(o_dn.shape[1]), row(o_m.shape[1]),
                  colw(w_sb.shape[0]), colw(w_dn.shape[0]), colw(w_m.shape[0]),
                  gate, gate, gate],
        out_specs=pl.BlockSpec((tm, tn), lambda j, i: (i, j)),
        out_shape=jax.ShapeDtypeStruct((m, d), BF16),
        compiler_params=_params(2),
        name="branch_merge",
    )(o_sb, o_dn, o_m, w_sb, w_dn, w_m, *gates)


def _layernorm_body(r_ref, g_ref, b_ref, o_ref):
    r = r_ref[...]
    mu = jnp.mean(r, axis=-1, keepdims=True)
    rc = r - mu
    var = jnp.mean(rc * rc, axis=-1, keepdims=True)
    o_ref[...] = rc * lax.rsqrt(var + LN_EPS) * g_ref[...] + b_ref[...]


def _layernorm(r, g, b, tm):
    m, d = r.shape
    return pl.pallas_call(
        _layernorm_body,
        grid=(m // tm,),
        in_specs=[pl.BlockSpec((tm, d), lambda i: (i, 0)),
                  pl.BlockSpec((1, d), lambda i: (0, 0)),
                  pl.BlockSpec((1, d), lambda i: (0, 0))],
        out_specs=pl.BlockSpec((tm, d), lambda i: (i, 0)),
        out_shape=jax.ShapeDtypeStruct((m, d), F32),
        compiler_params=_params(1),
        name="post_layernorm",
    )(r, g.reshape(1, d), b.reshape(1, d))


def _sb_body(q_ref, k_ref, v_ref, z_ref, tri_ref, o_ref, vt_ref, acc_ref, lsig_ref, sp_ref, *, tk, nq, hps):
    qi = pl.program_id(2)
    heads = range(hps)
    lanes = lambda hh: slice(hh * HEAD_DIM, (hh + 1) * HEAD_DIM)

    @pl.when(qi == 0)
    def _():
        def transpose_block(c, _):
            blk = v_ref[pl.ds(pl.multiple_of(c * tk, tk), tk), :].astype(F32)
            for hh in heads:
                vt_ref[hh, c] = blk[:, lanes(hh)].T.astype(BF16)
            return 0
        lax.fori_loop(0, nq, transpose_block, 0)

    qs = [q_ref[:, lanes(hh)] for hh in heads]
    key_pos = lax.broadcasted_iota(jnp.int32, (tk, tk), 0)
    qry_pos = lax.broadcasted_iota(jnp.int32, (tk, tk), 1)
    before = key_pos < qry_pos

    def score_dots(j):
        kb = k_ref[pl.ds(pl.multiple_of(j * tk, tk), tk), :]
        return [_dot_nt(kb[:, lanes(hh)], qs[hh]) for hh in heads]

    def scores_store(zts, slot, diagonal):
        for hh in heads:
            sp = _softplus2(zts[hh])
            lsig = zts[hh] - sp
            if diagonal:
                sp = jnp.where(before, sp, 0.0)
                lsig = jnp.where(before, lsig, MASKED_LOG2_WEIGHT)
            lsig_ref[slot, hh] = lsig
            sp_ref[slot, hh] = sp.astype(BF16)

    def suffix_dots(slot):
        return [_dot(tri_ref[...], sp_ref[slot, hh]) for hh in heads]

    def finish(j, slot, sufs, carries):
        out = []
        for hh in heads:
            w = jnp.exp2(lsig_ref[slot, hh] - sufs[hh][:tk] - carries[hh])
            acc_ref[hh] += _dot(vt_ref[hh, j], w.astype(BF16))
            out.append(carries[hh] + sufs[hh][tk:tk + 1])
        return tuple(out)

    def step(state):
        t, carries, _ = state
        slot = t & 1
        sufs = suffix_dots(slot)
        zts = score_dots(qi - t - 1)
        carries = finish(qi - t, slot, sufs, carries)
        scores_store(zts, 1 - slot, False)
        smallest = functools.reduce(jnp.minimum, carries)
        return t + 1, carries, jnp.min(smallest) < SB_UNDERFLOW_LOG2

    acc_ref[...] = jnp.zeros(acc_ref.shape, F32)
    scores_store(score_dots(qi), 0, True)
    t, carries, live = lax.while_loop(lambda s: jnp.logical_and(s[0] < qi, s[2]), step,
                                      (jnp.int32(0), (jnp.zeros((1, tk), F32),) * hps, jnp.bool_(True)))

    @pl.when(live)
    def _():
        last = t & 1
        finish(qi - t, last, suffix_dots(last), carries)

    for hh in heads:
        z = z_ref[:, lanes(hh)].astype(F32)
        o_ref[:, lanes(hh)] = (acc_ref[hh].T * (z * _sigmoid(z))).astype(o_ref.dtype)


def _sb_attention(h, batch, seq, heads, col0):
    t = _tile(seq, SB_BLOCK)
    nq = seq // t
    hps = SB_HEADS_PER_STEP if heads % SB_HEADS_PER_STEP == 0 and col0 % SB_HEADS_PER_STEP == 0 else 1
    ng = heads // hps
    c0 = col0 // hps
    w = hps * HEAD_DIM
    tri = np.zeros((t + SB_TRI_EXTRA_ROWS, t), np.float32)
    tri[:t] = np.triu(np.ones((t, t), np.float32), 1)
    tri[t] = 1.0
    tri = jnp.asarray(tri, BF16)
    body = functools.partial(_sb_body, tk=t, nq=nq, hps=hps)
    whole_seq = lambda g: pl.BlockSpec((seq, w), lambda b, h_, i: (b, c0 + g * ng + h_), pipeline_mode=pl.Buffered(1))
    return pl.pallas_call(
        body,
        grid=(batch, ng, nq),
        in_specs=[pl.BlockSpec((t, w), lambda b, h_, i: (b * nq + i, c0 + h_)),
                  whole_seq(1), whole_seq(2),
                  pl.BlockSpec((t, w), lambda b, h_, i: (b * nq + i, c0 + 3 * ng + h_)),
                  pl.BlockSpec((t + SB_TRI_EXTRA_ROWS, t), lambda b, h_, i: (0, 0))],
        out_specs=pl.BlockSpec((t, w), lambda b, h_, i: (b * nq + i, h_)),
        out_shape=jax.ShapeDtypeStruct((batch * seq, heads * HEAD_DIM), BF16),
        scratch_shapes=[pltpu.VMEM((hps, nq, HEAD_DIM, t), BF16), pltpu.VMEM((hps, HEAD_DIM, t), F32),
                        pltpu.VMEM((2, hps, t, t), F32), pltpu.VMEM((2, hps, t, t), BF16)],
        compiler_params=_params(3),
        name="stickbreak_attention",
    )(h, h, h, h, tri)


_M_INCL, _M_STRICT, _M_PAIR, _M_LEVEL0 = 0, 1, 2, 3
_N_LEVELS = int(np.log2(DN_CHUNK)) - 1
_M_EYE = _M_LEVEL0 + _N_LEVELS
_M_UPPER = _M_EYE + 1
_N_MASKS = _M_UPPER + 1
_DN_STAGES = 2 * _N_LEVELS + 4
GATE_K_PIECES = 4
assert _DN_STAGES % GATE_K_PIECES == 0


def _dn_masks():
    c = DN_CHUNK
    i = np.arange(c)[:, None]
    j = np.arange(c)[None, :]
    masks = [i >= j, i > j, (i % 2 == 1) & (j == i - 1)]
    b = 4
    while b <= c:
        masks.append((i // b == j // b) & (i % b >= b // 2) & (j % b < b // 2))
        b *= 2
    masks += [i == j, i <= j]
    assert len(masks) == _N_MASKS
    return np.stack(masks).astype(np.float32)


def _split3(x):
    p1 = x.astype(BF16)
    r = x - p1.astype(F32)
    p2 = r.astype(BF16)
    p3 = (r - p2.astype(F32)).astype(BF16)
    return p1, p2, p3


def _dn_body(q_ref, k_ref, v_ref, z_ref, cq_ref, ck_ref, cv_ref, b_ref, a_ref, alog_ref, dtb_ref, nw_ref, msk_ref,
             x_ref, wg0_ref, wg1_ref, wg2_ref, bg0_ref, bg1_ref, bg2_ref,
             o_ref, g0_ref, g1_ref, g2_ref,
             gc_ref, beta_ref, gl_ref, xf_ref, lhs_ref, add_ref, state_ref, acc_ref, *, n_groups):
    c = DN_CHUNK
    rows = DN_GROUP * c
    g = pl.program_id(2)
    wg_refs, bg_refs, gate_refs = (wg0_ref, wg1_ref, wg2_ref), (bg0_ref, bg1_ref, bg2_ref), (g0_ref, g1_ref, g2_ref)
    k_piece = x_ref.shape[1] // GATE_K_PIECES

    @pl.when(g == 0)
    def _():
        gd = -jnp.exp(alog_ref[0]) * _softplus(a_ref[0, 0] + dtb_ref[0])
        upper = msk_ref[_M_UPPER].astype(BF16)
        ones = jnp.ones((c, c), BF16)
        pieces = _split3(gd)
        gc_ref[...] = sum(_dot(p, upper) for p in pieces)
        gl_ref[...] = sum(_dot(p, ones) for p in pieces)
        beta_ref[...] = _sigmoid(b_ref[0, 0])
        state_ref[...] = jnp.zeros(state_ref.shape, F32)

    def gate_piece(i):
        ks = slice(i * k_piece, (i + 1) * k_piece)
        for t in range(3):
            part = _dot(x_ref[:, ks], wg_refs[t][ks, :])
            if i == 0:
                acc_ref[t] = part
            elif i < GATE_K_PIECES - 1:
                acc_ref[t] += part
            else:
                gate_refs[t][...] = _sigmoid(acc_ref[t] + part + bg_refs[t][...]).astype(gate_refs[t].dtype)

    def conv_silu(idx, x_blk_ref, cw_ref, first):
        if first:
            xf_ref[idx, 0:CONV_HALO, :] = jnp.zeros((CONV_HALO, HEAD_DIM), F32)
        else:
            xf_ref[idx, 0:CONV_HALO, :] = xf_ref[idx, rows:rows + CONV_HALO, :]
        xf_ref[idx, CONV_HALO:, :] = x_blk_ref[...].astype(F32)
        cw = cw_ref[...]
        y = None
        for s in range(CONV_K):
            tap = xf_ref[idx, CONV_HALO - s:CONV_HALO - s + rows, :] * cw[CONV_K - 1 - s:CONV_K - s, :]
            y = tap if y is None else y + tap
        return y * _sigmoid(y)

    def l2norm(t):
        return t * lax.rsqrt(jnp.sum(t * t, axis=-1, keepdims=True) + L2_EPS)

    def local_work(first, per_stage):
        stage = [0]

        def stage_done():
            for fn in per_stage[stage[0]]:
                fn()
            stage[0] += 1

        slot = g & 1
        qg = l2norm(conv_silu(0, q_ref, cq_ref, first)) * (HEAD_DIM ** -0.5)
        kg = l2norm(conv_silu(1, k_ref, ck_ref, first))
        vg = conv_silu(2, v_ref, cv_ref, first)
        cs = range(DN_GROUP)
        sl = lambda ci: slice(ci * c, (ci + 1) * c)
        qs, ks, vs = [qg[sl(ci)] for ci in cs], [kg[sl(ci)] for ci in cs], [vg[sl(ci)] for ci in cs]
        gls, gc_cols, beta_cols, decays = [], [], [], []
        for ci in cs:
            chunk = g * DN_GROUP + ci
            gc_row = jnp.broadcast_to(gc_ref[pl.ds(chunk, 1), :], (c, c))
            gc_col = gc_row.T
            incl = msk_ref[_M_INCL]
            gls.append(gl_ref[pl.ds(chunk, 1), :])
            gc_cols.append(gc_col)
            beta_cols.append(jnp.broadcast_to(beta_ref[pl.ds(chunk, 1), :], (c, c)).T)
            decays.append(jnp.exp((gc_col - gc_row) * incl) * incl)
        kbs = [ks[ci] * beta_cols[ci] for ci in cs]
        prods = [_dot_nt(jnp.concatenate([qs[ci], kbs[ci]], axis=0).astype(BF16), ks[ci].astype(BF16)) for ci in cs]
        stage_done()
        a_ins = [prods[ci][:c] * decays[ci] for ci in cs]
        a_los = [prods[ci][c:] * decays[ci] * msk_ref[_M_STRICT] for ci in cs]
        t_invs = [msk_ref[_M_EYE] - a_lo * msk_ref[_M_PAIR] for a_lo in a_los]
        for lvl in range(_N_LEVELS):
            offs = [(a_lo * msk_ref[_M_LEVEL0 + lvl]).astype(BF16) for a_lo in a_los]
            tbs = [t_inv.astype(BF16) for t_inv in t_invs]
            xs = [_dot(tbs[ci], offs[ci]).astype(BF16) for ci in cs]
            stage_done()
            t_invs = [t_invs[ci] - _dot(xs[ci], tbs[ci]) for ci in cs]
            stage_done()
        eg_cols = [jnp.exp(gc_col) for gc_col in gc_cols]
        rhss = [jnp.concatenate([kbs[ci] * eg_cols[ci], vs[ci] * beta_cols[ci]], axis=1).astype(BF16) for ci in cs]
        wus = [_dot(t_invs[ci].astype(BF16), rhss[ci]).astype(BF16) for ci in cs]
        stage_done()
        k_ends = [ks[ci] * jnp.exp(gls[ci] - gc_cols[ci]) for ci in cs]
        pns = [_dot(k_ends[ci].T.astype(BF16), wus[ci]) for ci in cs]
        stage_done()
        aw_aus = [_dot(a_ins[ci].astype(BF16), wus[ci]) for ci in cs]
        stage_done()
        assert stage[0] == _DN_STAGES == len(per_stage)
        for ci in cs:
            q_eff = qs[ci] * eg_cols[ci] - aw_aus[ci][:, :HEAD_DIM]
            lhs_ref[slot, ci] = jnp.concatenate([q_eff, pns[ci][:, :HEAD_DIM]], axis=0).astype(BF16)
            add_ref[slot, ci] = jnp.concatenate([aw_aus[ci][:, HEAD_DIM:], pns[ci][:, HEAD_DIM:]], axis=0)

    def scan_chunk(slot, ci, state):
        both = _dot(lhs_ref[slot, ci], state.astype(BF16))
        o = both[:c] + add_ref[slot, ci, :c]
        gl = gl_ref[pl.ds((g - 1) * DN_GROUP + ci, 1), :]
        state = state * jnp.exp(gl) - both[c:] + add_ref[slot, ci, c:]
        o = o * lax.rsqrt(jnp.mean(o * o, axis=-1, keepdims=True) + RMS_EPS) * nw_ref[...]
        zr = z_ref[ci * c:(ci + 1) * c, :].astype(F32)
        o_ref[ci * c:(ci + 1) * c, :] = (o * (zr * _sigmoid(zr))).astype(o_ref.dtype)
        return state

    def run(first, with_scan):
        box = [state_ref[...]] if with_scan else None

        def scan_hook(ci):
            box[0] = scan_chunk(1 - (g & 1), ci, box[0])

        per_stage = [[] for _ in range(_DN_STAGES)]
        for i in range(GATE_K_PIECES):
            per_stage[i * (_DN_STAGES // GATE_K_PIECES)].append(functools.partial(gate_piece, i))
        if with_scan:
            for ci in range(DN_GROUP):
                per_stage[2 * ci + 1].append(functools.partial(scan_hook, ci))
        local_work(first, per_stage)
        if with_scan:
            state_ref[...] = box[0]

    @pl.when(g == 0)
    def _():
        run(True, False)

    @pl.when(jnp.logical_and(g > 0, g < n_groups))
    def _():
        run(False, True)

    @pl.when(g == n_groups)
    def _():
        state = state_ref[...]
        for ci in range(DN_GROUP):
            state = scan_chunk((n_groups - 1) & 1, ci, state)


def _deltanet_and_gates(h, conv_w, ba, a_log, dt_bias, norm_w, xb, w_gate, b_gate, batch, seq, heads, col0):
    c = DN_CHUNK
    rows = DN_GROUP * c
    assert seq % rows == 0 and w_gate.shape[0] == 3
    n_chunks, ng = seq // c, seq // rows
    m, d = xb.shape
    gw = d // heads
    assert gw * heads == d and gw % HEAD_DIM == 0 and d % GATE_K_PIECES == 0
    masks = jnp.asarray(_dn_masks())
    lane = lambda v: jnp.broadcast_to(v.astype(F32)[:, None, None], (heads, 1, c))
    body = functools.partial(_dn_body, n_groups=ng)
    cur = lambda b, g: b * ng + jnp.minimum(g, ng - 1)
    prev = lambda b, g: b * ng + jnp.maximum(g - 1, 0)
    hcol = lambda t: pl.BlockSpec((rows, HEAD_DIM), lambda b, h_, g: (cur(b, g), col0 + t * heads + h_))
    cw = lambda t: pl.BlockSpec((CONV_K, HEAD_DIM), lambda b, h_, g: (0, t * heads + h_))
    wgs = lambda t: pl.BlockSpec((None, d, gw), lambda b, h_, g: (t, 0, h_))
    bgs = lambda t: pl.BlockSpec((None, 1, gw), lambda b, h_, g: (t, 0, h_))
    b_gate = b_gate.reshape(3, 1, d)
    gate_out = pl.BlockSpec((rows, gw), lambda b, h_, g: (cur(b, g), h_))
    outs = pl.pallas_call(
        body,
        grid=(batch, heads, ng + 1),
        in_specs=[hcol(0), hcol(1), hcol(2),
                  pl.BlockSpec((rows, HEAD_DIM), lambda b, h_, g: (prev(b, g), col0 + 3 * heads + h_)),
                  cw(0), cw(1), cw(2),
                  pl.BlockSpec((1, 1, n_chunks, c), lambda b, h_, g: (b, h_, 0, 0)),
                  pl.BlockSpec((1, 1, n_chunks, c), lambda b, h_, g: (b, heads + h_, 0, 0)),
                  pl.BlockSpec((1, 1, c), lambda b, h_, g: (h_, 0, 0)),
                  pl.BlockSpec((1, 1, c), lambda b, h_, g: (h_, 0, 0)),
                  pl.BlockSpec((1, HEAD_DIM), lambda b, h_, g: (0, 0)),
                  pl.BlockSpec((_N_MASKS, c, c), lambda b, h_, g: (0, 0, 0)),
                  pl.BlockSpec((rows, d), lambda b, h_, g: (cur(b, g), 0)),
                  wgs(0), wgs(1), wgs(2), bgs(0), bgs(1), bgs(2)],
        out_specs=[pl.BlockSpec((rows, HEAD_DIM), lambda b, h_, g: (prev(b, g), h_)), gate_out, gate_out, gate_out],
        out_shape=[jax.ShapeDtypeStruct((m, heads * HEAD_DIM), BF16)] + [jax.ShapeDtypeStruct((m, d), BF16)] * 3,
        scratch_shapes=[pltpu.VMEM((n_chunks, c), F32)] * 3
        + [pltpu.VMEM((3, CONV_HALO + rows, HEAD_DIM), F32),
           pltpu.VMEM((2, DN_GROUP, 2 * c, HEAD_DIM), BF16), pltpu.VMEM((2, DN_GROUP, 2 * c, HEAD_DIM), F32),
           pltpu.VMEM((HEAD_DIM, HEAD_DIM), F32), pltpu.VMEM((3, rows, gw), F32)],
        compiler_params=_params(3),
        name="deltanet_and_gates",
    )(h, h, h, h, conv_w, conv_w, conv_w, ba, ba, lane(a_log), lane(dt_bias), norm_w.reshape(1, HEAD_DIM), masks,
      xb, w_gate, w_gate, w_gate, b_gate, b_gate, b_gate)
    return outs[0], outs[1:]


def _mem_body(q_ref, z_ref, kv_ref, o_ref, *, width):
    for hh in range(width // HEAD_DIM):
        sl = slice(hh * HEAD_DIM, (hh + 1) * HEAD_DIM)
        q = (q_ref[:, sl] * (HEAD_DIM ** -0.5)).astype(BF16)
        k = kv_ref[:, sl]
        v = kv_ref[:, width + hh * HEAD_DIM:width + (hh + 1) * HEAD_DIM]
        s = _dot_nt(q, k)
        p = jnp.exp(s - jnp.max(s, axis=-1, keepdims=True))
        o = _dot(p.astype(BF16), v) * (1.0 / jnp.sum(p, axis=-1, keepdims=True))
        z = z_ref[:, sl]
        o_ref[:, sl] = (o * (z * _sigmoid(z))).astype(o_ref.dtype)


def _mem_attention(h_tail, kv, batch, seq, n_mem, width, tq):
    nq = seq // tq
    body = functools.partial(_mem_body, width=width)
    return pl.pallas_call(
        body,
        grid=(batch, nq),
        in_specs=[pl.BlockSpec((tq, width), lambda b, i: (b * nq + i, 0)),
                  pl.BlockSpec((tq, width), lambda b, i: (b * nq + i, 1)),
                  pl.BlockSpec((n_mem, 2 * width), lambda b, i: (b, 0))],
        out_specs=pl.BlockSpec((tq, width), lambda b, i: (b * nq + i, 0)),
        out_shape=jax.ShapeDtypeStruct((batch * seq, width), BF16),
        compiler_params=_params(2),
        name="memory_attention",
    )(h_tail, h_tail, kv)


def _layer_tiles(m, d, seq, n_main, mem_rows, mem_cols):
    return dict(rows=_tile(m, 1024), in_cols=_tile(n_main, 1024), out_cols=_tile(d, 512), tail_rows=_tile(m, 512),
                mem_rows=_tile(mem_rows, 512), mem_cols=_tile(mem_cols, 1024), mem_queries=_tile(seq, 512),
                norm_rows=_tile(m, 256))


def _layer(x, mem, w_in, conv_w, a_log, dt_bias, dn_norm_w, w_mem_kv, w_gate, b_gate,
           w_up_sb, w_up_dn, w_up_mem, w_out, ln_g, ln_b, alpha):
    batch, seq, d = x.shape
    n_mem = mem.shape[1]
    m = batch * seq
    sbw, dnw, memw = w_up_sb.shape[0], w_up_dn.shape[0], w_up_mem.shape[0]
    sbh, dnh = sbw // HEAD_DIM, dnw // HEAD_DIM
    n_main = 4 * sbw + 4 * dnw
    ba0 = n_main
    qm0 = ba0 + 2 * dnh
    assert w_in.shape[1] == qm0 + 2 * memw

    x2 = x.reshape(m, d)
    tiles = _layer_tiles(m, d, seq, n_main, batch * n_mem, 2 * memw)
    tm = tiles["rows"]

    w_in_b = w_in.astype(BF16)
    pad = (-2 * dnh) % HEAD_DIM
    w_tail = jnp.concatenate([w_in_b[:, qm0:], w_in_b[:, ba0:qm0], jnp.zeros((d, pad), BF16)], axis=1)
    h_tail, xb = _tail_projection(x2, w_tail, tiles["tail_rows"])
    col_scale = jnp.concatenate([jnp.full((sbw,), HEAD_DIM ** -0.5 * LOG2E, F32), jnp.ones((n_main - sbw,), F32)])
    h = _tiled_matmul(_proj_scale_body, xb, w_in_b,
                      [(col_scale.reshape(1, n_main), (1, tiles["in_cols"]), lambda j, i: (0, j))],
                      BF16, tm, tiles["in_cols"], "input_projection", n_cols=n_main)

    o_sb = _sb_attention(h, batch, seq, sbh, 0)

    ba = h_tail[:, 2 * memw:2 * memw + 2 * dnh].reshape(batch, seq, 2 * dnh)
    ba = ba.transpose(0, 2, 1).reshape(batch, 2 * dnh, seq // DN_CHUNK, DN_CHUNK)
    o_dn, gates = _deltanet_and_gates(h, conv_w, ba, a_log, dt_bias, dn_norm_w, xb, w_gate.astype(BF16), b_gate,
                                      batch, seq, dnh, 4 * sbh)

    kv = _tiled_matmul(_proj_body, mem.reshape(batch * n_mem, d).astype(BF16), w_mem_kv.astype(BF16), [], BF16,
                       tiles["mem_rows"], tiles["mem_cols"], "memory_kv_projection")
    o_m = _mem_attention(h_tail, kv, batch, seq, n_mem, memw, tiles["mem_queries"])

    y = _merge(o_sb, o_dn, o_m, w_up_sb.astype(BF16), w_up_dn.astype(BF16), w_up_mem.astype(BF16), gates, tm,
               tiles["out_cols"])
    r = _tiled_matmul(functools.partial(_resid_body, alpha=alpha), y, w_out.astype(BF16),
                      [(x2, (tm, tiles["out_cols"]), lambda j, i: (i, j))], F32, tm, tiles["out_cols"],
                      "output_projection")
    return _layernorm(r, ln_g, ln_b, tiles["norm_rows"]).reshape(batch, seq, d)


def kernel(x, mem, w_in, conv_w, a_log, dt_bias, dn_norm_w, w_mem_kv, w_gate, b_gate, w_up_sb, w_up_dn, w_up_mem,
           w_out, ln_g, ln_b):
    depth = w_in.shape[0]
    alpha = (2.0 * depth) ** 0.25
    for l in range(depth):
        x = _layer(x, mem, w_in[l], conv_w[l], a_log[l], dt_bias[l], dn_norm_w[l], w_mem_kv[l], w_gate[l],
                   b_gate[l], w_up_sb[l], w_up_dn[l], w_up_mem[l], w_out[l], ln_g[l], ln_b[l], alpha)
    return x
```

```python
import functools

import numpy as np
import jax
import jax.numpy as jnp
from jax import lax
from jax.experimental import pallas as pl
from jax.experimental.pallas import tpu as pltpu

HEAD_DIM = 128
CONV_K = 4
CONV_HALO = 8
LN_EPS = 1e-5
RMS_EPS = 1e-6
L2_EPS = 1e-6
LOG2E = 1.4426950408889634
SOFTPLUS2_CLAMP = 64.0
MASKED_LOG2_WEIGHT = -1e30
SB_TRI_EXTRA_ROWS = 16
SB_UNDERFLOW_LOG2 = 1100.0

DN_CHUNK = 128
DN_GROUP = 8
SB_BLOCK = 256
SB_HEADS_PER_STEP = 4
V7X_VMEM_LIMIT_BYTES = 56 * 1024 * 1024

F32 = jnp.float32
BF16 = jnp.bfloat16


def _params(n_axes):
    return pltpu.CompilerParams(dimension_semantics=("arbitrary",) * n_axes,
                                vmem_limit_bytes=V7X_VMEM_LIMIT_BYTES)


def _dot(a, b):
    return jnp.dot(a, b, preferred_element_type=F32)


def _dot_nt(a, b):
    return lax.dot_general(a, b, (((1,), (1,)), ((), ())), preferred_element_type=F32)


def _sigmoid(x):
    return 1.0 / (1.0 + jnp.exp(-x))


def _softplus(x):
    return jnp.maximum(x, 0.0) + jnp.log(1.0 + jnp.exp(-jnp.abs(x)))


def _softplus2(x):
    return jnp.maximum(x, jnp.log(1.0 + jnp.exp2(jnp.minimum(x, SOFTPLUS2_CLAMP))) * LOG2E)


def _tile(n, pref):
    t = min(n, pref)
    assert n % t == 0, (n, t)
    return t


def _with_weight_cast(body):
    def wrapped(a_ref, w32_ref, *rest):
        *others, wb_ref = rest

        @pl.when(pl.program_id(1) == 0)
        def _():
            wb_ref[...] = w32_ref[...].astype(BF16)

        body(a_ref, wb_ref, *others)
    return wrapped


def _tiled_matmul(body, a, b, extra, out_dtype, tm, tn, name, n_cols=None, single_buffer_b=False):
    m, k = a.shape
    n = b.shape[1] if n_cols is None else n_cols
    assert m % tm == 0 and n % tn == 0
    cast = b.dtype == F32
    mode = dict(pipeline_mode=pl.Buffered(1)) if (single_buffer_b or cast) else {}
    in_specs = [pl.BlockSpec((tm, k), lambda j, i: (i, 0)), pl.BlockSpec((k, tn), lambda j, i: (0, j), **mode)]
    in_specs += [pl.BlockSpec(blk, imap) for _, blk, imap in extra]
    return pl.pallas_call(
        _with_weight_cast(body) if cast else body,
        grid=(n // tn, m // tm),
        in_specs=in_specs,
        out_specs=pl.BlockSpec((tm, tn), lambda j, i: (i, j)),
        out_shape=jax.ShapeDtypeStruct((m, n), out_dtype),
        scratch_shapes=[pltpu.VMEM((k, tn), BF16)] if cast else [],
        compiler_params=_params(2),
        name=name,
    )(a, b, *[e[0] for e in extra])


def _proj_scale_body(a_ref, b_ref, s_ref, o_ref):
    o_ref[...] = (_dot(a_ref[...], b_ref[...]) * s_ref[...]).astype(o_ref.dtype)


def _tail_body(x_ref, w_ref, o_ref, xb_ref):
    xb = x_ref[...].astype(BF16)
    xb_ref[...] = xb
    o_ref[...] = _dot(xb, w_ref[...])


def _tail_projection(x2, w_tail, tm):
    m, d = x2.shape
    n = w_tail.shape[1]
    return pl.pallas_call(
        _tail_body,
        grid=(m // tm,),
        in_specs=[pl.BlockSpec((tm, d), lambda i: (i, 0)),
                  pl.BlockSpec((d, n), lambda i: (0, 0), pipeline_mode=pl.Buffered(1))],
        out_specs=[pl.BlockSpec((tm, n), lambda i: (i, 0)), pl.BlockSpec((tm, d), lambda i: (i, 0))],
        out_shape=[jax.ShapeDtypeStruct((m, n), F32), jax.ShapeDtypeStruct((m, d), BF16)],
        compiler_params=_params(1),
        name="input_projection_tail",
    )(x2, w_tail)


def _proj_body(a_ref, b_ref, o_ref):
    o_ref[...] = _dot(a_ref[...], b_ref[...]).astype(o_ref.dtype)


def _resid_body(a_ref, b_ref, x_ref, o_ref, *, alpha):
    o_ref[...] = alpha * x_ref[...] + _dot(a_ref[...], b_ref[...])


def _merge_body(osb_ref, odn_ref, om_ref, wsb_ref, wdn_ref, wm_ref, g0_ref, g1_ref, g2_ref, o_ref):
    y = g0_ref[...].astype(F32) * _dot(osb_ref[...], wsb_ref[...])
    y += g1_ref[...].astype(F32) * _dot(odn_ref[...], wdn_ref[...])
    y += g2_ref[...].astype(F32) * _dot(om_ref[...], wm_ref[...])
    o_ref[...] = y.astype(o_ref.dtype)


def _merge(o_sb, o_dn, o_m, w_sb, w_dn, w_m, gates, tm, tn):
    m = o_sb.shape[0]
    d = w_sb.shape[1]
    row = lambda w: pl.BlockSpec((tm, w), lambda j, i: (i, 0))
    colw = lambda w: pl.BlockSpec((w, tn), lambda j, i: (0, j), pipeline_mode=pl.Buffered(1))
    gate = pl.BlockSpec((tm, tn), lambda j, i: (i, j))
    return pl.pallas_call(
        _merge_body,
        grid=(d // tn, m // tm),
        in_specs=[row(o_sb.shape[1]), row(o_dn.shape[1]), row(o_m.shape[1]),
                  colw(w_sb.shape[0]), colw(w_dn.shape[0]), colw(w_m.shape[0]),
                  gate, gate, gate],
        out_specs=pl.BlockSpec((tm, tn), lambda j, i: (i, j)),
        out_shape=jax.ShapeDtypeStruct((m, d), BF16),
        compiler_params=_params(2),
        name="branch_merge",
    )(o_sb, o_dn, o_m, w_sb, w_dn, w_m, *gates)


def _layernorm_body(r_ref, g_ref, b_ref, o_ref):
    r = r_ref[...]
    mu = jnp.mean(r, axis=-1, keepdims=True)
    rc = r - mu
    var = jnp.mean(rc * rc, axis=-1, keepdims=True)
    o_ref[...] = rc * lax.rsqrt(var + LN_EPS) * g_ref[...] + b_ref[...]


def _layernorm(r, g, b, tm):
    m, d = r.shape
    return pl.pallas_call(
        _layernorm_body,
        grid=(m // tm,),
        in_specs=[pl.BlockSpec((tm, d), lambda i: (i, 0)),
                  pl.BlockSpec((1, d), lambda i: (0, 0)),
                  pl.BlockSpec((1, d), lambda i: (0, 0))],
        out_specs=pl.BlockSpec((tm, d), lambda i: (i, 0)),
        out_shape=jax.ShapeDtypeStruct((m, d), F32),
        compiler_params=_params(1),
        name="post_layernorm",
    )(r, g.reshape(1, d), b.reshape(1, d))


def _sb_body(q_ref, k_ref, v_ref, z_ref, tri_ref, o_ref, vt_ref, acc_ref, lsig_ref, sp_ref, *, tk, nq, hps):
    qi = pl.program_id(2)
    heads = range(hps)
    lanes = lambda hh: slice(hh * HEAD_DIM, (hh + 1) * HEAD_DIM)

    @pl.when(qi == 0)
    def _():
        def transpose_block(c, _):
            blk = v_ref[pl.ds(pl.multiple_of(c * tk, tk), tk), :].astype(F32)
            for hh in heads:
                vt_ref[hh, c] = blk[:, lanes(hh)].T.astype(BF16)
            return 0
        lax.fori_loop(0, nq, transpose_block, 0)

    qs = [q_ref[:, lanes(hh)] for hh in heads]
    key_pos = lax.broadcasted_iota(jnp.int32, (tk, tk), 0)
    qry_pos = lax.broadcasted_iota(jnp.int32, (tk, tk), 1)
    before = key_pos < qry_pos

    def score_dots(j):
        kb = k_ref[pl.ds(pl.multiple_of(j * tk, tk), tk), :]
        return [_dot_nt(kb[:, lanes(hh)], qs[hh]) for hh in heads]

    def scores_store(zts, slot, diagonal):
        for hh in heads:
            sp = _softplus2(zts[hh])
            lsig = zts[hh] - sp
            if diagonal:
                sp = jnp.where(before, sp, 0.0)
                lsig = jnp.where(before, lsig, MASKED_LOG2_WEIGHT)
            lsig_ref[slot, hh] = lsig
            sp_ref[slot, hh] = sp.astype(BF16)

    def suffix_dots(slot):
        return [_dot(tri_ref[...], sp_ref[slot, hh]) for hh in heads]

    def finish(j, slot, sufs, carries):
        out = []
        for hh in heads:
            w = jnp.exp2(lsig_ref[slot, hh] - sufs[hh][:tk] - carries[hh])
            acc_ref[hh] += _dot(vt_ref[hh, j], w.astype(BF16))
            out.append(carries[hh] + sufs[hh][tk:tk + 1])
        return tuple(out)

    def step(state):
        t, carries, _ = state
        slot = t & 1
        sufs = suffix_dots(slot)
        zts = score_dots(qi - t - 1)
        carries = finish(qi - t, slot, sufs, carries)
        scores_store(zts, 1 - slot, False)
        smallest = functools.reduce(jnp.minimum, carries)
        return t + 1, carries, jnp.min(smallest) < SB_UNDERFLOW_LOG2

    acc_ref[...] = jnp.zeros(acc_ref.shape, F32)
    scores_store(score_dots(qi), 0, True)
    t, carries, live = lax.while_loop(lambda s: jnp.logical_and(s[0] < qi, s[2]), step,
                                      (jnp.int32(0), (jnp.zeros((1, tk), F32),) * hps, jnp.bool_(True)))

    @pl.when(live)
    def _():
        last = t & 1
        finish(qi - t, last, suffix_dots(last), carries)

    for hh in heads:
        z = z_ref[:, lanes(hh)].astype(F32)
        o_ref[:, lanes(hh)] = (acc_ref[hh].T * (z * _sigmoid(z))).astype(o_ref.dtype)


def _sb_attention(h, batch, seq, heads, col0):
    t = _tile(seq, SB_BLOCK)
    nq = seq // t
    hps = SB_HEADS_PER_STEP if heads % SB_HEADS_PER_STEP == 0 and col0 % SB_HEADS_PER_STEP == 0 else 1
    ng = heads // hps
    c0 = col0 // hps
    w = hps * HEAD_DIM
    tri = np.zeros((t + SB_TRI_EXTRA_ROWS, t), np.float32)
    tri[:t] = np.triu(np.ones((t, t), np.float32), 1)
    tri[t] = 1.0
    tri = jnp.asarray(tri, BF16)
    body = functools.partial(_sb_body, tk=t, nq=nq, hps=hps)
    whole_seq = lambda g: pl.BlockSpec((seq, w), lambda b, h_, i: (b, c0 + g * ng + h_), pipeline_mode=pl.Buffered(1))
    return pl.pallas_call(
        body,
        grid=(batch, ng, nq),
        in_specs=[pl.BlockSpec((t, w), lambda b, h_, i: (b * nq + i, c0 + h_)),
                  whole_seq(1), whole_seq(2),
                  pl.BlockSpec((t, w), lambda b, h_, i: (b * nq + i, c0 + 3 * ng + h_)),
                  pl.BlockSpec((t + SB_TRI_EXTRA_ROWS, t), lambda b, h_, i: (0, 0))],
        out_specs=pl.BlockSpec((t, w), lambda b, h_, i: (b * nq + i, h_)),
        out_shape=jax.ShapeDtypeStruct((batch * seq, heads * HEAD_DIM), BF16),
        scratch_shapes=[pltpu.VMEM((hps, nq, HEAD_DIM, t), BF16), pltpu.VMEM((hps, HEAD_DIM, t), F32),
                        pltpu.VMEM((2, hps, t, t), F32), pltpu.VMEM((2, hps, t, t), BF16)],
        compiler_params=_params(3),
        name="stickbreak_attention",
    )(h, h, h, h, tri)


_M_INCL, _M_STRICT, _M_PAIR, _M_LEVEL0 = 0, 1, 2, 3
_N_LEVELS = int(np.log2(DN_CHUNK)) - 1
_M_EYE = _M_LEVEL0 + _N_LEVELS
_M_UPPER = _M_EYE + 1
_N_MASKS = _M_UPPER + 1
_DN_STAGES = 2 * _N_LEVELS + 4
GATE_K_PIECES = 4
assert _DN_STAGES % GATE_K_PIECES == 0


def _dn_masks():
    c = DN_CHUNK
    i = np.arange(c)[:, None]
    j = np.arange(c)[None, :]
    masks = [i >= j, i > j, (i % 2 == 1) & (j == i - 1)]
    b = 4
    while b <= c:
        masks.append((i // b == j // b) & (i % b >= b // 2) & (j % b < b // 2))
        b *= 2
    masks += [i == j, i <= j]
    assert len(masks) == _N_MASKS
    return np.stack(masks).astype(np.float32)


def _split3(x):
    p1 = x.astype(BF16)
    r = x - p1.astype(F32)
    p2 = r.astype(BF16)
    p3 = (r - p2.astype(F32)).astype(BF16)
    return p1, p2, p3


def _dn_body(q_ref, k_ref, v_ref, z_ref, cq_ref, ck_ref, cv_ref, b_ref, a_ref, alog_ref, dtb_ref, nw_ref, msk_ref,
             x_ref, wg0_ref, wg1_ref, wg2_ref, bg0_ref, bg1_ref, bg2_ref,
             o_ref, g0_ref, g1_ref, g2_ref,
             gc_ref, beta_ref, gl_ref, xf_ref, lhs_ref, add_ref, state_ref, acc_ref, *, n_groups):
    c = DN_CHUNK
    rows = DN_GROUP * c
    g = pl.program_id(2)
    wg_refs, bg_refs, gate_refs = (wg0_ref, wg1_ref, wg2_ref), (bg0_ref, bg1_ref, bg2_ref), (g0_ref, g1_ref, g2_ref)
    k_piece = x_ref.shape[1] // GATE_K_PIECES

    @pl.when(g == 0)
    def _():
        gd = -jnp.exp(alog_ref[0]) * _softplus(a_ref[0, 0] + dtb_ref[0])
        upper = msk_ref[_M_UPPER].astype(BF16)
        ones = jnp.ones((c, c), BF16)
        pieces = _split3(gd)
        gc_ref[...] = sum(_dot(p, upper) for p in pieces)
        gl_ref[...] = sum(_dot(p, ones) for p in pieces)
        beta_ref[...] = _sigmoid(b_ref[0, 0])
        state_ref[...] = jnp.zeros(state_ref.shape, F32)

    def gate_piece(i):
        ks = slice(i * k_piece, (i + 1) * k_piece)
        for t in range(3):
            part = _dot(x_ref[:, ks], wg_refs[t][ks, :])
            if i == 0:
                acc_ref[t] = part
            elif i < GATE_K_PIECES - 1:
                acc_ref[t] += part
            else:
                gate_refs[t][...] = _sigmoid(acc_ref[t] + part + bg_refs[t][...]).astype(gate_refs[t].dtype)

    def conv_silu(idx, x_blk_ref, cw_ref, first):
        if first:
            xf_ref[idx, 0:CONV_HALO, :] = jnp.zeros((CONV_HALO, HEAD_DIM), F32)
        else:
            xf_ref[idx, 0:CONV_HALO, :] = xf_ref[idx, rows:rows + CONV_HALO, :]
        xf_ref[idx, CONV_HALO:, :] = x_blk_ref[...].astype(F32)
        cw = cw_ref[...]
        y = None
        for s in range(CONV_K):
            tap = xf_ref[idx, CONV_HALO - s:CONV_HALO - s + rows, :] * cw[CONV_K - 1 - s:CONV_K - s, :]
            y = tap if y is None else y + tap
        return y * _sigmoid(y)

    def l2norm(t):
        return t * lax.rsqrt(jnp.sum(t * t, axis=-1, keepdims=True) + L2_EPS)

    def local_work(first, per_stage):
        stage = [0]

        def stage_done():
            for fn in per_stage[stage[0]]:
                fn()
            stage[0] += 1

        slot = g & 1
        qg = l2norm(conv_silu(0, q_ref, cq_ref, first)) * (HEAD_DIM ** -0.5)
        kg = l2norm(conv_silu(1, k_ref, ck_ref, first))
        vg = conv_silu(2, v_ref, cv_ref, first)
        cs = range(DN_GROUP)
        sl = lambda ci: slice(ci * c, (ci + 1) * c)
        qs, ks, vs = [qg[sl(ci)] for ci in cs], [kg[sl(ci)] for ci in cs], [vg[sl(ci)] for ci in cs]
        gls, gc_cols, beta_cols, decays = [], [], [], []
        for ci in cs:
            chunk = g * DN_GROUP + ci
            gc_row = jnp.broadcast_to(gc_ref[pl.ds(chunk, 1), :], (c, c))
            gc_col = gc_row.T
            incl = msk_ref[_M_INCL]
            gls.append(gl_ref[pl.ds(chunk, 1), :])
            gc_cols.append(gc_col)
            beta_cols.append(jnp.broadcast_to(beta_ref[pl.ds(chunk, 1), :], (c, c)).T)
            decays.append(jnp.exp((gc_col - gc_row) * incl) * incl)
        kbs = [ks[ci] * beta_cols[ci] for ci in cs]
        prods = [_dot_nt(jnp.concatenate([qs[ci], kbs[ci]], axis=0).astype(BF16), ks[ci].astype(BF16)) for ci in cs]
        stage_done()
        a_ins = [prods[ci][:c] * decays[ci] for ci in cs]
        a_los = [prods[ci][c:] * decays[ci] * msk_ref[_M_STRICT] for ci in cs]
        t_invs = [msk_ref[_M_EYE] - a_lo * msk_ref[_M_PAIR] for a_lo in a_los]
        for lvl in range(_N_LEVELS):
            offs = [(a_lo * msk_ref[_M_LEVEL0 + lvl]).astype(BF16) for a_lo in a_los]
            tbs = [t_inv.astype(BF16) for t_inv in t_invs]
            xs = [_dot(tbs[ci], offs[ci]).astype(BF16) for ci in cs]
            stage_done()
            t_invs = [t_invs[ci] - _dot(xs[ci], tbs[ci]) for ci in cs]
            stage_done()
        eg_cols = [jnp.exp(gc_col) for gc_col in gc_cols]
        rhss = [jnp.concatenate([kbs[ci] * eg_cols[ci], vs[ci] * beta_cols[ci]], axis=1).astype(BF16) for ci in cs]
        wus = [_dot(t_invs[ci].astype(BF16), rhss[ci]).astype(BF16) for ci in cs]
        stage_done()
        k_ends = [ks[ci] * jnp.exp(gls[ci] - gc_cols[ci]) for ci in cs]
        pns = [_dot(k_ends[ci].T.astype(BF16), wus[ci]) for ci in cs]
        stage_done()
        aw_aus = [_dot(a_ins[ci].astype(BF16), wus[ci]) for ci in cs]
        stage_done()
        assert stage[0] == _DN_STAGES == len(per_stage)
        for ci in cs:
            q_eff = qs[ci] * eg_cols[ci] - aw_aus[ci][:, :HEAD_DIM]
            lhs_ref[slot, ci] = jnp.concatenate([q_eff, pns[ci][:, :HEAD_DIM]], axis=0).astype(BF16)
            add_ref[slot, ci] = jnp.concatenate([aw_aus[ci][:, HEAD_DIM:], pns[ci][:, HEAD_DIM:]], axis=0)

    def scan_chunk(slot, ci, state):
        both = _dot(lhs_ref[slot, ci], state.astype(BF16))
        o = both[:c] + add_ref[slot, ci, :c]
        gl = gl_ref[pl.ds((g - 1) * DN_GROUP + ci, 1), :]
        state = state * jnp.exp(gl) - both[c:] + add_ref[slot, ci, c:]
        o = o * lax.rsqrt(jnp.mean(o * o, axis=-1, keepdims=True) + RMS_EPS) * nw_ref[...]
        zr = z_ref[ci * c:(ci + 1) * c, :].astype(F32)
        o_ref[ci * c:(ci + 1) * c, :] = (o * (zr * _sigmoid(zr))).astype(o_ref.dtype)
        return state

    def run(first, with_scan):
        box = [state_ref[...]] if with_scan else None

        def scan_hook(ci):
            box[0] = scan_chunk(1 - (g & 1), ci, box[0])

        per_stage = [[] for _ in range(_DN_STAGES)]
        for i in range(GATE_K_PIECES):
            per_stage[i * (_DN_STAGES // GATE_K_PIECES)].append(functools.partial(gate_piece, i))
        if with_scan:
            for ci in range(DN_GROUP):
                per_stage[2 * ci + 1].append(functools.partial(scan_hook, ci))
        local_work(first, per_stage)
        if with_scan:
            state_ref[...] = box[0]

    @pl.when(g == 0)
    def _():
        run(True, False)

    @pl.when(jnp.logical_and(g > 0, g < n_groups))
    def _():
        run(False, True)

    @pl.when(g == n_groups)
    def _():
        state = state_ref[...]
        for ci in range(DN_GROUP):
            state = scan_chunk((n_groups - 1) & 1, ci, state)


def _deltanet_and_gates(h, conv_w, ba, a_log, dt_bias, norm_w, xb, w_gate, b_gate, batch, seq, heads, col0):
    c = DN_CHUNK
    rows = DN_GROUP * c
    assert seq % rows == 0 and w_gate.shape[0] == 3
    n_chunks, ng = seq // c, seq // rows
    m, d = xb.shape
    gw = d // heads
    assert gw * heads == d and gw % HEAD_DIM == 0 and d % GATE_K_PIECES == 0
    masks = jnp.asarray(_dn_masks())
    lane = lambda v: jnp.broadcast_to(v.astype(F32)[:, None, None], (heads, 1, c))
    body = functools.partial(_dn_body, n_groups=ng)
    cur = lambda b, g: b * ng + jnp.minimum(g, ng - 1)
    prev = lambda b, g: b * ng + jnp.maximum(g - 1, 0)
    hcol = lambda t: pl.BlockSpec((rows, HEAD_DIM), lambda b, h_, g: (cur(b, g), col0 + t * heads + h_))
    cw = lambda t: pl.BlockSpec((CONV_K, HEAD_DIM), lambda b, h_, g: (0, t * heads + h_))
    wgs = lambda t: pl.BlockSpec((None, d, gw), lambda b, h_, g: (t, 0, h_))
    bgs = lambda t: pl.BlockSpec((None, 1, gw), lambda b, h_, g: (t, 0, h_))
    b_gate = b_gate.reshape(3, 1, d)
    gate_out = pl.BlockSpec((rows, gw), lambda b, h_, g: (cur(b, g), h_))
    outs = pl.pallas_call(
        body,
        grid=(batch, heads, ng + 1),
        in_specs=[hcol(0), hcol(1), hcol(2),
                  pl.BlockSpec((rows, HEAD_DIM), lambda b, h_, g: (prev(b, g), col0 + 3 * heads + h_)),
                  cw(0), cw(1), cw(2),
                  pl.BlockSpec((1, 1, n_chunks, c), lambda b, h_, g: (b, h_, 0, 0)),
                  pl.BlockSpec((1, 1, n_chunks, c), lambda b, h_, g: (b, heads + h_, 0, 0)),
                  pl.BlockSpec((1, 1, c), lambda b, h_, g: (h_, 0, 0)),
                  pl.BlockSpec((1, 1, c), lambda b, h_, g: (h_, 0, 0)),
                  pl.BlockSpec((1, HEAD_DIM), lambda b, h_, g: (0, 0)),
                  pl.BlockSpec((_N_MASKS, c, c), lambda b, h_, g: (0, 0, 0)),
                  pl.BlockSpec((rows, d), lambda b, h_, g: (cur(b, g), 0)),
                  wgs(0), wgs(1), wgs(2), bgs(0), bgs(1), bgs(2)],
        out_specs=[pl.BlockSpec((rows, HEAD_DIM), lambda b, h_, g: (prev(b, g), h_)), gate_out, gate_out, gate_out],
        out_shape=[jax.ShapeDtypeStruct((m, heads * HEAD_DIM), BF16)] + [jax.ShapeDtypeStruct((m, d), BF16)] * 3,
        scratch_shapes=[pltpu.VMEM((n_chunks, c), F32)] * 3
        + [pltpu.VMEM((3, CONV_HALO + rows, HEAD_DIM), F32),
           pltpu.VMEM((2, DN_GROUP, 2 * c, HEAD_DIM), BF16), pltpu.VMEM((2, DN_GROUP, 2 * c, HEAD_DIM), F32),
           pltpu.VMEM((HEAD_DIM, HEAD_DIM), F32), pltpu.VMEM((3, rows, gw), F32)],
        compiler_params=_params(3),
        name="deltanet_and_gates",
    )(h, h, h, h, conv_w, conv_w, conv_w, ba, ba, lane(a_log), lane(dt_bias), norm_w.reshape(1, HEAD_DIM), masks,
      xb, w_gate, w_gate, w_gate, b_gate, b_gate, b_gate)
    return outs[0], outs[1:]


def _mem_body(q_ref, z_ref, kv_ref, o_ref, *, width):
    for hh in range(width // HEAD_DIM):
        sl = slice(hh * HEAD_DIM, (hh + 1) * HEAD_DIM)
        q = (q_ref[:, sl] * (HEAD_DIM ** -0.5)).astype(BF16)
        k = kv_ref[:, sl]
        v = kv_ref[:, width + hh * HEAD_DIM:width + (hh + 1) * HEAD_DIM]
        s = _dot_nt(q, k)
        p = jnp.exp(s - jnp.max(s, axis=-1, keepdims=True))
        o = _dot(p.astype(BF16), v) * (1.0 / jnp.sum(p, axis=-1, keepdims=True))
        z = z_ref[:, sl]
        o_ref[:, sl] = (o * (z * _sigmoid(z))).astype(o_ref.dtype)


def _mem_attention(h_tail, kv, batch, seq, n_mem, width, tq):
    nq = seq // tq
    body = functools.partial(_mem_body, width=width)
    return pl.pallas_call(
        body,
        grid=(batch, nq),
        in_specs=[pl.BlockSpec((tq, width), lambda b, i: (b * nq + i, 0)),
                  pl.BlockSpec((tq, width), lambda b, i: (b * nq + i, 1)),
                  pl.BlockSpec((n_mem, 2 * width), lambda b, i: (b, 0))],
        out_specs=pl.BlockSpec((tq, width), lambda b, i: (b * nq + i, 0)),
        out_shape=jax.ShapeDtypeStruct((batch * seq, width), BF16),
        compiler_params=_params(2),
        name="memory_attention",
    )(h_tail, h_tail, kv)


def _layer_tiles(m, d, seq, n_main, mem_rows, mem_cols):
    return dict(rows=_tile(m, 1024), in_cols=_tile(n_main, 1024), out_cols=_tile(d, 1024), tail_rows=_tile(m, 512),
                mem_rows=_tile(mem_rows, 512), mem_cols=_tile(mem_cols, 1024), mem_queries=_tile(seq, 512),
                norm_rows=_tile(m, 256))


def _layer(x, mem, w_in, conv_w, a_log, dt_bias, dn_norm_w, w_mem_kv, w_gate, b_gate,
           w_up_sb, w_up_dn, w_up_mem, w_out, ln_g, ln_b, alpha):
    batch, seq, d = x.shape
    n_mem = mem.shape[1]
    m = batch * seq
    sbw, dnw, memw = w_up_sb.shape[0], w_up_dn.shape[0], w_up_mem.shape[0]
    sbh, dnh = sbw // HEAD_DIM, dnw // HEAD_DIM
    n_main = 4 * sbw + 4 * dnw
    ba0 = n_main
    qm0 = ba0 + 2 * dnh
    assert w_in.shape[1] == qm0 + 2 * memw

    x2 = x.reshape(m, d)
    tiles = _layer_tiles(m, d, seq, n_main, batch * n_mem, 2 * memw)
    tm = tiles["rows"]

    pad = (-2 * dnh) % HEAD_DIM
    w_tail = jnp.concatenate([w_in[:, qm0:], w_in[:, ba0:qm0], jnp.zeros((d, pad), F32)], axis=1).astype(BF16)
    h_tail, xb = _tail_projection(x2, w_tail, tiles["tail_rows"])
    col_scale = jnp.concatenate([jnp.full((sbw,), HEAD_DIM ** -0.5 * LOG2E, F32), jnp.ones((n_main - sbw,), F32)])
    h = _tiled_matmul(_proj_scale_body, xb, w_in,
                      [(col_scale.reshape(1, n_main), (1, tiles["in_cols"]), lambda j, i: (0, j))],
                      BF16, tm, tiles["in_cols"], "input_projection", n_cols=n_main)

    o_sb = _sb_attention(h, batch, seq, sbh, 0)

    ba = h_tail[:, 2 * memw:2 * memw + 2 * dnh].reshape(batch, seq, 2 * dnh)
    ba = ba.transpose(0, 2, 1).reshape(batch, 2 * dnh, seq // DN_CHUNK, DN_CHUNK)
    o_dn, gates = _deltanet_and_gates(h, conv_w, ba, a_log, dt_bias, dn_norm_w, xb, w_gate.astype(BF16), b_gate,
                                      batch, seq, dnh, 4 * sbh)

    kv = _tiled_matmul(_proj_body, mem.reshape(batch * n_mem, d).astype(BF16), w_mem_kv.astype(BF16), [], BF16,
                       tiles["mem_rows"], tiles["mem_cols"], "memory_kv_projection")
    o_m = _mem_attention(h_tail, kv, batch, seq, n_mem, memw, tiles["mem_queries"])

    y = _merge(o_sb, o_dn, o_m, w_up_sb.astype(BF16), w_up_dn.astype(BF16), w_up_mem.astype(BF16), gates, tm,
               tiles["out_cols"])
    r = _tiled_matmul(functools.partial(_resid_body, alpha=alpha), y, w_out.astype(BF16),
                      [(x2, (tm, tiles["out_cols"]), lambda j, i: (i, j))], F32, tm, tiles["out_cols"],
                      "output_projection", single_buffer_b=True)
    return _layernorm(r, ln_g, ln_b, tiles["norm_rows"]).reshape(batch, seq, d)


def kernel(x, mem, w_in, conv_w, a_log, dt_bias, dn_norm_w, w_mem_kv, w_gate, b_gate, w_up_sb, w_up_dn, w_up_mem,
           w_out, ln_g, ln_b):
    depth = w_in.shape[0]
    alpha = (2.0 * depth) ** 0.25
    for l in range(depth):
        x = _layer(x, mem, w_in[l], conv_w[l], a_log[l], dt_bias[l], dn_norm_w[l], w_mem_kv[l], w_gate[l],
                   b_gate[l], w_up_sb[l], w_up_dn[l], w_up_mem[l], w_out[l], ln_g[l], ln_b[l], alpha)
    return x
```

```python
import functools

import numpy as np
import jax
import jax.numpy as jnp
from jax import lax
from jax.experimental import pallas as pl
from jax.experimental.pallas import tpu as pltpu

HEAD_DIM = 128
CONV_K = 4
CONV_HALO = 8
LN_EPS = 1e-5
RMS_EPS = 1e-6
L2_EPS = 1e-6
LOG2E = 1.4426950408889634
SOFTPLUS2_CLAMP = 64.0
MASKED_LOG2_WEIGHT = -1e30
SB_TRI_EXTRA_ROWS = 16
SB_UNDERFLOW_LOG2 = 1100.0

DN_CHUNK = 128
DN_GROUP = 8
SB_BLOCK = 256
SB_HEADS_PER_STEP = 4
V7X_VMEM_LIMIT_BYTES = 48 * 1024 * 1024

F32 = jnp.float32
BF16 = jnp.bfloat16


def _params(n_axes):
    return pltpu.CompilerParams(dimension_semantics=("arbitrary",) * n_axes,
                                vmem_limit_bytes=V7X_VMEM_LIMIT_BYTES)


def _dot(a, b):
    return jnp.dot(a, b, preferred_element_type=F32)


def _dot_nt(a, b):
    return lax.dot_general(a, b, (((1,), (1,)), ((), ())), preferred_element_type=F32)


def _sigmoid(x):
    return 1.0 / (1.0 + jnp.exp(-x))


def _softplus(x):
    return jnp.maximum(x, 0.0) + jnp.log(1.0 + jnp.exp(-jnp.abs(x)))


def _softplus2(x):
    return jnp.maximum(x, jnp.log(1.0 + jnp.exp2(jnp.minimum(x, SOFTPLUS2_CLAMP))) * LOG2E)


def _tile(n, pref):
    t = min(n, pref)
    assert n % t == 0, (n, t)
    return t


def _tiled_matmul(body, a, b, extra, out_dtype, tm, tn, name, n_cols=None):
    m, k = a.shape
    n = b.shape[1] if n_cols is None else n_cols
    assert m % tm == 0 and n % tn == 0
    in_specs = [pl.BlockSpec((tm, k), lambda j, i: (i, 0)), pl.BlockSpec((k, tn), lambda j, i: (0, j))]
    in_specs += [pl.BlockSpec(blk, imap) for _, blk, imap in extra]
    return pl.pallas_call(
        body,
        grid=(n // tn, m // tm),
        in_specs=in_specs,
        out_specs=pl.BlockSpec((tm, tn), lambda j, i: (i, j)),
        out_shape=jax.ShapeDtypeStruct((m, n), out_dtype),
        compiler_params=_params(2),
        name=name,
    )(a, b, *[e[0] for e in extra])


def _proj_scale_body(a_ref, b_ref, s_ref, o_ref):
    o_ref[...] = (_dot(a_ref[...], b_ref[...]) * s_ref[...]).astype(o_ref.dtype)


def _tail_body(x_ref, w_ref, o_ref, xb_ref):
    xb = x_ref[...].astype(BF16)
    xb_ref[...] = xb
    o_ref[...] = _dot(xb, w_ref[...])


def _tail_projection(x2, w_tail, tm):
    m, d = x2.shape
    n = w_tail.shape[1]
    return pl.pallas_call(
        _tail_body,
        grid=(m // tm,),
        in_specs=[pl.BlockSpec((tm, d), lambda i: (i, 0)),
                  pl.BlockSpec((d, n), lambda i: (0, 0), pipeline_mode=pl.Buffered(1))],
        out_specs=[pl.BlockSpec((tm, n), lambda i: (i, 0)), pl.BlockSpec((tm, d), lambda i: (i, 0))],
        out_shape=[jax.ShapeDtypeStruct((m, n), F32), jax.ShapeDtypeStruct((m, d), BF16)],
        compiler_params=_params(1),
        name="input_projection_tail",
    )(x2, w_tail)


def _proj_body(a_ref, b_ref, o_ref):
    o_ref[...] = _dot(a_ref[...], b_ref[...]).astype(o_ref.dtype)


def _resid_body(a_ref, b_ref, x_ref, o_ref, *, alpha):
    o_ref[...] = alpha * x_ref[...] + _dot(a_ref[...], b_ref[...])


def _merge_body(osb_ref, odn_ref, om_ref, wsb_ref, wdn_ref, wm_ref, g0_ref, g1_ref, g2_ref, o_ref):
    y = g0_ref[...].astype(F32) * _dot(osb_ref[...], wsb_ref[...])
    y += g1_ref[...].astype(F32) * _dot(odn_ref[...], wdn_ref[...])
    y += g2_ref[...].astype(F32) * _dot(om_ref[...], wm_ref[...])
    o_ref[...] = y.astype(o_ref.dtype)


def _merge(o_sb, o_dn, o_m, w_sb, w_dn, w_m, gates, tm, tn):
    m = o_sb.shape[0]
    d = w_sb.shape[1]
    row = lambda w: pl.BlockSpec((tm, w), lambda j, i: (i, 0))
    colw = lambda w: pl.BlockSpec((w, tn), lambda j, i: (0, j))
    gate = pl.BlockSpec((tm, tn), lambda j, i: (i, j))
    return pl.pallas_call(
        _merge_body,
        grid=(d // tn, m // tm),
        in_specs=[row(o_sb.shape[1]), row(o_dn.shape[1]), row(o_m.shape[1]),
                  colw(w_sb.shape[0]), colw(w_dn.shape[0]), colw(w_m.shape[0]),
                  gate, gate, gate],
        out_specs=pl.BlockSpec((tm, tn), lambda j, i: (i, j)),
        out_shape=jax.ShapeDtypeStruct((m, d), BF16),
        compiler_params=_params(2),
        name="branch_merge",
    )(o_sb, o_dn, o_m, w_sb, w_dn, w_m, *gates)


def _layernorm_body(r_ref, g_ref, b_ref, o_ref):
    r = r_ref[...]
    mu = jnp.mean(r, axis=-1, keepdims=True)
    rc = r - mu
    var = jnp.mean(rc * rc, axis=-1, keepdims=True)
    o_ref[...] = rc * lax.rsqrt(var + LN_EPS) * g_ref[...] + b_ref[...]


def _layernorm(r, g, b, tm):
    m, d = r.shape
    return pl.pallas_call(
        _layernorm_body,
        grid=(m // tm,),
        in_specs=[pl.BlockSpec((tm, d), lambda i: (i, 0)),
                  pl.BlockSpec((1, d), lambda i: (0, 0)),
                  pl.BlockSpec((1, d), lambda i: (0, 0))],
        out_specs=pl.BlockSpec((tm, d), lambda i: (i, 0)),
        out_shape=jax.ShapeDtypeStruct((m, d), F32),
        compiler_params=_params(1),
        name="post_layernorm",
    )(r, g.reshape(1, d), b.reshape(1, d))


def _sb_body(q_ref, k_ref, v_ref, z_ref, tri_ref, o_ref, vt_ref, acc_ref, lsig_ref, sp_ref, *, tk, nq, hps):
    qi = pl.program_id(2)
    heads = range(hps)
    lanes = lambda hh: slice(hh * HEAD_DIM, (hh + 1) * HEAD_DIM)

    @pl.when(qi == 0)
    def _():
        def transpose_block(c, _):
            blk = v_ref[pl.ds(pl.multiple_of(c * tk, tk), tk), :].astype(F32)
            for hh in heads:
                vt_ref[hh, c] = blk[:, lanes(hh)].T.astype(BF16)
            return 0
        lax.fori_loop(0, nq, transpose_block, 0)

    qs = [q_ref[:, lanes(hh)] for hh in heads]
    key_pos = lax.broadcasted_iota(jnp.int32, (tk, tk), 0)
    qry_pos = lax.broadcasted_iota(jnp.int32, (tk, tk), 1)
    before = key_pos < qry_pos

    def score_dots(j):
        kb = k_ref[pl.ds(pl.multiple_of(j * tk, tk), tk), :]
        return [_dot_nt(kb[:, lanes(hh)], qs[hh]) for hh in heads]

    def scores_store(zts, slot, diagonal):
        for hh in heads:
            sp = _softplus2(zts[hh])
            lsig = zts[hh] - sp
            if diagonal:
                sp = jnp.where(before, sp, 0.0)
                lsig = jnp.where(before, lsig, MASKED_LOG2_WEIGHT)
            lsig_ref[slot, hh] = lsig
            sp_ref[slot, hh] = sp.astype(BF16)

    def suffix_dots(slot):
        return [_dot(tri_ref[...], sp_ref[slot, hh]) for hh in heads]

    def finish(j, slot, sufs, carries):
        out = []
        for hh in heads:
            w = jnp.exp2(lsig_ref[slot, hh] - sufs[hh][:tk] - carries[hh])
            acc_ref[hh] += _dot(vt_ref[hh, j], w.astype(BF16))
            out.append(carries[hh] + sufs[hh][tk:tk + 1])
        return tuple(out)

    def step(state):
        t, carries, _ = state
        slot = t & 1
        sufs = suffix_dots(slot)
        zts = score_dots(qi - t - 1)
        carries = finish(qi - t, slot, sufs, carries)
        scores_store(zts, 1 - slot, False)
        smallest = functools.reduce(jnp.minimum, carries)
        return t + 1, carries, jnp.min(smallest) < SB_UNDERFLOW_LOG2

    acc_ref[...] = jnp.zeros(acc_ref.shape, F32)
    scores_store(score_dots(qi), 0, True)
    t, carries, live = lax.while_loop(lambda s: jnp.logical_and(s[0] < qi, s[2]), step,
                                      (jnp.int32(0), (jnp.zeros((1, tk), F32),) * hps, jnp.bool_(True)))

    @pl.when(live)
    def _():
        last = t & 1
        finish(qi - t, last, suffix_dots(last), carries)

    for hh in heads:
        z = z_ref[:, lanes(hh)].astype(F32)
        o_ref[:, lanes(hh)] = (acc_ref[hh].T * (z * _sigmoid(z))).astype(o_ref.dtype)


def _sb_attention(h, batch, seq, heads, col0):
    t = _tile(seq, SB_BLOCK)
    nq = seq // t
    hps = SB_HEADS_PER_STEP if heads % SB_HEADS_PER_STEP == 0 and col0 % SB_HEADS_PER_STEP == 0 else 1
    ng = heads // hps
    c0 = col0 // hps
    w = hps * HEAD_DIM
    tri = np.zeros((t + SB_TRI_EXTRA_ROWS, t), np.float32)
    tri[:t] = np.triu(np.ones((t, t), np.float32), 1)
    tri[t] = 1.0
    tri = jnp.asarray(tri, BF16)
    body = functools.partial(_sb_body, tk=t, nq=nq, hps=hps)
    whole_seq = lambda g: pl.BlockSpec((seq, w), lambda b, h_, i: (b, c0 + g * ng + h_), pipeline_mode=pl.Buffered(1))
    return pl.pallas_call(
        body,
        grid=(batch, ng, nq),
        in_specs=[pl.BlockSpec((t, w), lambda b, h_, i: (b * nq + i, c0 + h_)),
                  whole_seq(1), whole_seq(2),
                  pl.BlockSpec((t, w), lambda b, h_, i: (b * nq + i, c0 + 3 * ng + h_)),
                  pl.BlockSpec((t + SB_TRI_EXTRA_ROWS, t), lambda b, h_, i: (0, 0))],
        out_specs=pl.BlockSpec((t, w), lambda b, h_, i: (b * nq + i, h_)),
        out_shape=jax.ShapeDtypeStruct((batch * seq, heads * HEAD_DIM), BF16),
        scratch_shapes=[pltpu.VMEM((hps, nq, HEAD_DIM, t), BF16), pltpu.VMEM((hps, HEAD_DIM, t), F32),
                        pltpu.VMEM((2, hps, t, t), F32), pltpu.VMEM((2, hps, t, t), BF16)],
        compiler_params=_params(3),
        name="stickbreak_attention",
    )(h, h, h, h, tri)


_M_INCL, _M_STRICT, _M_PAIR, _M_LEVEL0 = 0, 1, 2, 3
_N_LEVELS = int(np.log2(DN_CHUNK)) - 1
_M_EYE = _M_LEVEL0 + _N_LEVELS
_M_UPPER = _M_EYE + 1
_N_MASKS = _M_UPPER + 1
_DN_STAGES = 2 * _N_LEVELS + 4
GATE_K_PIECES = 4
assert _DN_STAGES % GATE_K_PIECES == 0


def _dn_masks():
    c = DN_CHUNK
    i = np.arange(c)[:, None]
    j = np.arange(c)[None, :]
    masks = [i >= j, i > j, (i % 2 == 1) & (j == i - 1)]
    b = 4
    while b <= c:
        masks.append((i // b == j // b) & (i % b >= b // 2) & (j % b < b // 2))
        b *= 2
    masks += [i == j, i <= j]
    assert len(masks) == _N_MASKS
    return np.stack(masks).astype(np.float32)


def _split3(x):
    p1 = x.astype(BF16)
    r = x - p1.astype(F32)
    p2 = r.astype(BF16)
    p3 = (r - p2.astype(F32)).astype(BF16)
    return p1, p2, p3


def _dn_body(q_ref, k_ref, v_ref, z_ref, cq_ref, ck_ref, cv_ref, b_ref, a_ref, alog_ref, dtb_ref, nw_ref, msk_ref,
             x_ref, wg0_ref, wg1_ref, wg2_ref, bg0_ref, bg1_ref, bg2_ref,
             o_ref, g0_ref, g1_ref, g2_ref,
             gc_ref, beta_ref, gl_ref, xf_ref, lhs_ref, add_ref, state_ref, acc_ref, *, n_groups):
    c = DN_CHUNK
    rows = DN_GROUP * c
    g = pl.program_id(2)
    wg_refs, bg_refs, gate_refs = (wg0_ref, wg1_ref, wg2_ref), (bg0_ref, bg1_ref, bg2_ref), (g0_ref, g1_ref, g2_ref)
    k_piece = x_ref.shape[1] // GATE_K_PIECES

    @pl.when(g == 0)
    def _():
        gd = -jnp.exp(alog_ref[0]) * _softplus(a_ref[0, 0] + dtb_ref[0])
        upper = msk_ref[_M_UPPER].astype(BF16)
        ones = jnp.ones((c, c), BF16)
        pieces = _split3(gd)
        gc_ref[...] = sum(_dot(p, upper) for p in pieces)
        gl_ref[...] = sum(_dot(p, ones) for p in pieces)
        beta_ref[...] = _sigmoid(b_ref[0, 0])
        state_ref[...] = jnp.zeros(state_ref.shape, F32)

    def gate_piece(i):
        ks = slice(i * k_piece, (i + 1) * k_piece)
        for t in range(3):
            part = _dot(x_ref[:, ks], wg_refs[t][ks, :])
            if i == 0:
                acc_ref[t] = part
            elif i < GATE_K_PIECES - 1:
                acc_ref[t] += part
            else:
                gate_refs[t][...] = _sigmoid(acc_ref[t] + part + bg_refs[t][...]).astype(gate_refs[t].dtype)

    def conv_silu(idx, x_blk_ref, cw_ref, first):
        if first:
            xf_ref[idx, 0:CONV_HALO, :] = jnp.zeros((CONV_HALO, HEAD_DIM), F32)
        else:
            xf_ref[idx, 0:CONV_HALO, :] = xf_ref[idx, rows:rows + CONV_HALO, :]
        xf_ref[idx, CONV_HALO:, :] = x_blk_ref[...].astype(F32)
        cw = cw_ref[...]
        y = None
        for s in range(CONV_K):
            tap = xf_ref[idx, CONV_HALO - s:CONV_HALO - s + rows, :] * cw[CONV_K - 1 - s:CONV_K - s, :]
            y = tap if y is None else y + tap
        return y * _sigmoid(y)

    def l2norm(t):
        return t * lax.rsqrt(jnp.sum(t * t, axis=-1, keepdims=True) + L2_EPS)

    def local_work(first, per_stage):
        stage = [0]

        def stage_done():
            for fn in per_stage[stage[0]]:
                fn()
            stage[0] += 1

        slot = g & 1
        qg = l2norm(conv_silu(0, q_ref, cq_ref, first)) * (HEAD_DIM ** -0.5)
        kg = l2norm(conv_silu(1, k_ref, ck_ref, first))
        vg = conv_silu(2, v_ref, cv_ref, first)
        cs = range(DN_GROUP)
        sl = lambda ci: slice(ci * c, (ci + 1) * c)
        qs, ks, vs = [qg[sl(ci)] for ci in cs], [kg[sl(ci)] for ci in cs], [vg[sl(ci)] for ci in cs]
        gls, gc_cols, beta_cols, decays = [], [], [], []
        for ci in cs:
            chunk = g * DN_GROUP + ci
            gc_row = jnp.broadcast_to(gc_ref[pl.ds(chunk, 1), :], (c, c))
            gc_col = gc_row.T
            incl = msk_ref[_M_INCL]
            gls.append(gl_ref[pl.ds(chunk, 1), :])
            gc_cols.append(gc_col)
            beta_cols.append(jnp.broadcast_to(beta_ref[pl.ds(chunk, 1), :], (c, c)).T)
            decays.append(jnp.exp((gc_col - gc_row) * incl) * incl)
        kbs = [ks[ci] * beta_cols[ci] for ci in cs]
        prods = [_dot_nt(jnp.concatenate([qs[ci], kbs[ci]], axis=0).astype(BF16), ks[ci].astype(BF16)) for ci in cs]
        stage_done()
        a_ins = [prods[ci][:c] * decays[ci] for ci in cs]
        a_los = [prods[ci][c:] * decays[ci] * msk_ref[_M_STRICT] for ci in cs]
        t_invs = [msk_ref[_M_EYE] - a_lo * msk_ref[_M_PAIR] for a_lo in a_los]
        for lvl in range(_N_LEVELS):
            offs = [(a_lo * msk_ref[_M_LEVEL0 + lvl]).astype(BF16) for a_lo in a_los]
            tbs = [t_inv.astype(BF16) for t_inv in t_invs]
            xs = [_dot(tbs[ci], offs[ci]).astype(BF16) for ci in cs]
            stage_done()
            t_invs = [t_invs[ci] - _dot(xs[ci], tbs[ci]) for ci in cs]
            stage_done()
        eg_cols = [jnp.exp(gc_col) for gc_col in gc_cols]
        rhss = [jnp.concatenate([kbs[ci] * eg_cols[ci], vs[ci] * beta_cols[ci]], axis=1).astype(BF16) for ci in cs]
        wus = [_dot(t_invs[ci].astype(BF16), rhss[ci]).astype(BF16) for ci in cs]
        stage_done()
        k_ends = [ks[ci] * jnp.exp(gls[ci] - gc_cols[ci]) for ci in cs]
        pns = [_dot(k_ends[ci].T.astype(BF16), wus[ci]) for ci in cs]
        stage_done()
        aw_aus = [_dot(a_ins[ci].astype(BF16), wus[ci]) for ci in cs]
        stage_done()
        assert stage[0] == _DN_STAGES == len(per_stage)
        for ci in cs:
            q_eff = qs[ci] * eg_cols[ci] - aw_aus[ci][:, :HEAD_DIM]
            lhs_ref[slot, ci] = jnp.concatenate([q_eff, pns[ci][:, :HEAD_DIM]], axis=0).astype(BF16)
            add_ref[slot, ci] = jnp.concatenate([aw_aus[ci][:, HEAD_DIM:], pns[ci][:, HEAD_DIM:]], axis=0)

    def scan_chunk(slot, ci, state):
        both = _dot(lhs_ref[slot, ci], state.astype(BF16))
        o = both[:c] + add_ref[slot, ci, :c]
        gl = gl_ref[pl.ds((g - 1) * DN_GROUP + ci, 1), :]
        state = state * jnp.exp(gl) - both[c:] + add_ref[slot, ci, c:]
        o = o * lax.rsqrt(jnp.mean(o * o, axis=-1, keepdims=True) + RMS_EPS) * nw_ref[...]
        zr = z_ref[ci * c:(ci + 1) * c, :].astype(F32)
        o_ref[ci * c:(ci + 1) * c, :] = (o * (zr * _sigmoid(zr))).astype(o_ref.dtype)
        return state

    def run(first, with_scan):
        box = [state_ref[...]] if with_scan else None

        def scan_hook(ci):
            box[0] = scan_chunk(1 - (g & 1), ci, box[0])

        per_stage = [[] for _ in range(_DN_STAGES)]
        for i in range(GATE_K_PIECES):
            per_stage[i * (_DN_STAGES // GATE_K_PIECES)].append(functools.partial(gate_piece, i))
        if with_scan:
            for ci in range(DN_GROUP):
                per_stage[2 * ci + 1].append(functools.partial(scan_hook, ci))
        local_work(first, per_stage)
        if with_scan:
            state_ref[...] = box[0]

    @pl.when(g == 0)
    def _():
        run(True, False)

    @pl.when(jnp.logical_and(g > 0, g < n_groups))
    def _():
        run(False, True)

    @pl.when(g == n_groups)
    def _():
        state = state_ref[...]
        for ci in range(DN_GROUP):
            state = scan_chunk((n_groups - 1) & 1, ci, state)


def _deltanet_and_gates(h, conv_w, ba, a_log, dt_bias, norm_w, xb, w_gate, b_gate, batch, seq, heads, col0):
    c = DN_CHUNK
    rows = DN_GROUP * c
    assert seq % rows == 0 and w_gate.shape[0] == 3
    n_chunks, ng = seq // c, seq // rows
    m, d = xb.shape
    gw = d // heads
    assert gw * heads == d and gw % HEAD_DIM == 0 and d % GATE_K_PIECES == 0
    masks = jnp.asarray(_dn_masks())
    lane = lambda v: jnp.broadcast_to(v.astype(F32)[:, None, None], (heads, 1, c))
    body = functools.partial(_dn_body, n_groups=ng)
    cur = lambda b, g: b * ng + jnp.minimum(g, ng - 1)
    prev = lambda b, g: b * ng + jnp.maximum(g - 1, 0)
    hcol = lambda t: pl.BlockSpec((rows, HEAD_DIM), lambda b, h_, g: (cur(b, g), col0 + t * heads + h_))
    cw = lambda t: pl.BlockSpec((CONV_K, HEAD_DIM), lambda b, h_, g: (0, t * heads + h_))
    wgs = lambda t: pl.BlockSpec((None, d, gw), lambda b, h_, g: (t, 0, h_))
    bgs = lambda t: pl.BlockSpec((None, 1, gw), lambda b, h_, g: (t, 0, h_))
    b_gate = b_gate.reshape(3, 1, d)
    gate_out = pl.BlockSpec((rows, gw), lambda b, h_, g: (cur(b, g), h_))
    outs = pl.pallas_call(
        body,
        grid=(batch, heads, ng + 1),
        in_specs=[hcol(0), hcol(1), hcol(2),
                  pl.BlockSpec((rows, HEAD_DIM), lambda b, h_, g: (prev(b, g), col0 + 3 * heads + h_)),
                  cw(0), cw(1), cw(2),
                  pl.BlockSpec((1, 1, n_chunks, c), lambda b, h_, g: (b, h_, 0, 0)),
                  pl.BlockSpec((1, 1, n_chunks, c), lambda b, h_, g: (b, heads + h_, 0, 0)),
                  pl.BlockSpec((1, 1, c), lambda b, h_, g: (h_, 0, 0)),
                  pl.BlockSpec((1, 1, c), lambda b, h_, g: (h_, 0, 0)),
                  pl.BlockSpec((1, HEAD_DIM), lambda b, h_, g: (0, 0)),
                  pl.BlockSpec((_N_MASKS, c, c), lambda b, h_, g: (0, 0, 0)),
                  pl.BlockSpec((rows, d), lambda b, h_, g: (cur(b, g), 0)),
                  wgs(0), wgs(1), wgs(2), bgs(0), bgs(1), bgs(2)],
        out_specs=[pl.BlockSpec((rows, HEAD_DIM), lambda b, h_, g: (prev(b, g), h_)), gate_out, gate_out, gate_out],
        out_shape=[jax.ShapeDtypeStruct((m, heads * HEAD_DIM), BF16)] + [jax.ShapeDtypeStruct((m, d), BF16)] * 3,
        scratch_shapes=[pltpu.VMEM((n_chunks, c), F32)] * 3
        + [pltpu.VMEM((3, CONV_HALO + rows, HEAD_DIM), F32),
           pltpu.VMEM((2, DN_GROUP, 2 * c, HEAD_DIM), BF16), pltpu.VMEM((2, DN_GROUP, 2 * c, HEAD_DIM), F32),
           pltpu.VMEM((HEAD_DIM, HEAD_DIM), F32), pltpu.VMEM((3, rows, gw), F32)],
        compiler_params=_params(3),
        name="deltanet_and_gates",
    )(h, h, h, h, conv_w, conv_w, conv_w, ba, ba, lane(a_log), lane(dt_bias), norm_w.reshape(1, HEAD_DIM), masks,
      xb, w_gate, w_gate, w_gate, b_gate, b_gate, b_gate)
    return outs[0], outs[1:]


def _mem_body(q_ref, z_ref, kv_ref, o_ref, *, width):
    for hh in range(width // HEAD_DIM):
        sl = slice(hh * HEAD_DIM, (hh + 1) * HEAD_DIM)
        q = (q_ref[:, sl] * (HEAD_DIM ** -0.5)).astype(BF16)
        k = kv_ref[:, sl]
        v = kv_ref[:, width + hh * HEAD_DIM:width + (hh + 1) * HEAD_DIM]
        s = _dot_nt(q, k)
        p = jnp.exp(s - jnp.max(s, axis=-1, keepdims=True))
        o = _dot(p.astype(BF16), v) * (1.0 / jnp.sum(p, axis=-1, keepdims=True))
        z = z_ref[:, sl]
        o_ref[:, sl] = (o * (z * _sigmoid(z))).astype(o_ref.dtype)


def _mem_attention(h_tail, kv, batch, seq, n_mem, width, tq):
    nq = seq // tq
    body = functools.partial(_mem_body, width=width)
    return pl.pallas_call(
        body,
        grid=(batch, nq),
        in_specs=[pl.BlockSpec((tq, width), lambda b, i: (b * nq + i, 0)),
                  pl.BlockSpec((tq, width), lambda b, i: (b * nq + i, 1)),
                  pl.BlockSpec((n_mem, 2 * width), lambda b, i: (b, 0))],
        out_specs=pl.BlockSpec((tq, width), lambda b, i: (b * nq + i, 0)),
        out_shape=jax.ShapeDtypeStruct((batch * seq, width), BF16),
        compiler_params=_params(2),
        name="memory_attention",
    )(h_tail, h_tail, kv)


def _layer_tiles(m, d, seq, n_main, mem_rows, mem_cols):
    return dict(rows=_tile(m, 1024), in_cols=_tile(n_main, 1024), out_cols=_tile(d, 512), tail_rows=_tile(m, 512),
                mem_rows=_tile(mem_rows, 512), mem_cols=_tile(mem_cols, 1024), mem_queries=_tile(seq, 512),
                norm_rows=_tile(m, 256))


def _layer(x, mem, w_in, conv_w, a_log, dt_bias, dn_norm_w, w_mem_kv, w_gate, b_gate,
           w_up_sb, w_up_dn, w_up_mem, w_out, ln_g, ln_b, alpha):
    batch, seq, d = x.shape
    n_mem = mem.shape[1]
    m = batch * seq
    sbw, dnw, memw = w_up_sb.shape[0], w_up_dn.shape[0], w_up_mem.shape[0]
    sbh, dnh = sbw // HEAD_DIM, dnw // HEAD_DIM
    n_main = 4 * sbw + 4 * dnw
    ba0 = n_main
    qm0 = ba0 + 2 * dnh
    assert w_in.shape[1] == qm0 + 2 * memw

    x2 = x.reshape(m, d)
    tiles = _layer_tiles(m, d, seq, n_main, batch * n_mem, 2 * memw)
    tm = tiles["rows"]

    w_in_b = w_in.astype(BF16)
    pad = (-2 * dnh) % HEAD_DIM
    w_tail = jnp.concatenate([w_in_b[:, qm0:], w_in_b[:, ba0:qm0], jnp.zeros((d, pad), BF16)], axis=1)
    h_tail, xb = _tail_projection(x2, w_tail, tiles["tail_rows"])
    col_scale = jnp.concatenate([jnp.full((sbw,), HEAD_DIM ** -0.5 * LOG2E, F32), jnp.ones((n_main - sbw,), F32)])
    h = _tiled_matmul(_proj_scale_body, xb, w_in_b,
                      [(col_scale.reshape(1, n_main), (1, tiles["in_cols"]), lambda j, i: (0, j))],
                      BF16, tm, tiles["in_cols"], "input_projection", n_cols=n_main)

    o_sb = _sb_attention(h, batch, seq, sbh, 0)

    ba = h_tail[:, 2 * memw:2 * memw + 2 * dnh].reshape(batch, seq, 2 * dnh)
    ba = ba.transpose(0, 2, 1).reshape(batch, 2 * dnh, seq // DN_CHUNK, DN_CHUNK)
    o_dn, gates = _deltanet_and_gates(h, conv_w, ba, a_log, dt_bias, dn_norm_w, xb, w_gate.astype(BF16), b_gate,
                                      batch, seq, dnh, 4 * sbh)

    kv = _tiled_matmul(_proj_body, mem.reshape(batch * n_mem, d).astype(BF16), w_mem_kv.astype(BF16), [], BF16,
                       tiles["mem_rows"], tiles["mem_cols"], "memory_kv_projection")
    o_m = _mem_attention(h_tail, kv, batch, seq, n_mem, memw, tiles["mem_queries"])

    y = _merge(o_sb, o_dn, o_m, w_up_sb.astype(BF16), w_up_dn.astype(BF16), w_up_mem.astype(BF16), gates, tm,
               tiles["out_cols"])
    r = _tiled_matmul(functools.partial(_resid_body, alpha=alpha), y, w_out.astype(BF16),
                      [(x2, (tm, tiles["out_cols"]), lambda j, i: (i, j))], F32, tm, tiles["out_cols"],
                      "output_projection")
    return _layernorm(r, ln_g, ln_b, tiles["norm_rows"]).reshape(batch, seq, d)


def kernel(x, mem, w_in, conv_w, a_log, dt_bias, dn_norm_w, w_mem_kv, w_gate, b_gate, w_up_sb, w_up_dn, w_up_mem,
           w_out, ln_g, ln_b):
    depth = w_in.shape[0]
    alpha = (2.0 * depth) ** 0.25
    for l in range(depth):
        x = _layer(x, mem, w_in[l], conv_w[l], a_log[l], dt_bias[l], dn_norm_w[l], w_mem_kv[l], w_gate[l],
                   b_gate[l], w_up_sb[l], w_up_dn[l], w_up_mem[l], w_out[l], ln_g[l], ln_b[l], alpha)
    return x
```

```python
import functools

import numpy as np
import jax
import jax.numpy as jnp
from jax import lax
from jax.experimental import pallas as pl
from jax.experimental.pallas import tpu as pltpu

HEAD_DIM = 128
CONV_K = 4
CONV_HALO = 8
LN_EPS = 1e-5
RMS_EPS = 1e-6
L2_EPS = 1e-6
LOG2E = 1.4426950408889634
SOFTPLUS2_CLAMP = 64.0
MASKED_LOG2_WEIGHT = -1e30
SB_TRI_EXTRA_ROWS = 16
SB_UNDERFLOW_LOG2 = 1100.0

DN_CHUNK = 128
DN_GROUP = 8
SB_BLOCK = 256
SB_HEADS_PER_STEP = 4
V7X_VMEM_LIMIT_BYTES = 48 * 1024 * 1024

F32 = jnp.float32
BF16 = jnp.bfloat16


def _params(n_axes):
    return pltpu.CompilerParams(dimension_semantics=("arbitrary",) * n_axes,
                                vmem_limit_bytes=V7X_VMEM_LIMIT_BYTES)


def _dot(a, b):
    return jnp.dot(a, b, preferred_element_type=F32)


def _dot_nt(a, b):
    return lax.dot_general(a, b, (((1,), (1,)), ((), ())), preferred_element_type=F32)


def _sigmoid(x):
    return 1.0 / (1.0 + jnp.exp(-x))


def _softplus(x):
    return jnp.maximum(x, 0.0) + jnp.log(1.0 + jnp.exp(-jnp.abs(x)))


def _softplus2(x):
    return jnp.maximum(x, jnp.log(1.0 + jnp.exp2(jnp.minimum(x, SOFTPLUS2_CLAMP))) * LOG2E)


def _tile(n, pref):
    t = min(n, pref)
    assert n % t == 0, (n, t)
    return t


def _tiled_matmul(body, a, b, extra, out_dtype, tm, tn, name, n_cols=None):
    m, k = a.shape
    n = b.shape[1] if n_cols is None else n_cols
    assert m % tm == 0 and n % tn == 0
    in_specs = [pl.BlockSpec((tm, k), lambda j, i: (i, 0)), pl.BlockSpec((k, tn), lambda j, i: (0, j))]
    in_specs += [pl.BlockSpec(blk, imap) for _, blk, imap in extra]
    return pl.pallas_call(
        body,
        grid=(n // tn, m // tm),
        in_specs=in_specs,
        out_specs=pl.BlockSpec((tm, tn), lambda j, i: (i, j)),
        out_shape=jax.ShapeDtypeStruct((m, n), out_dtype),
        compiler_params=_params(2),
        name=name,
    )(a, b, *[e[0] for e in extra])


def _proj_scale_body(a_ref, b_ref, s_ref, o_ref):
    o_ref[...] = (_dot(a_ref[...], b_ref[...]) * s_ref[...]).astype(o_ref.dtype)


def _tail_body(x_ref, w_ref, o_ref, xb_ref):
    xb = x_ref[...].astype(BF16)
    xb_ref[...] = xb
    o_ref[...] = _dot(xb, w_ref[...])


def _tail_projection(x2, w_tail, tm):
    m, d = x2.shape
    n = w_tail.shape[1]
    return pl.pallas_call(
        _tail_body,
        grid=(m // tm,),
        in_specs=[pl.BlockSpec((tm, d), lambda i: (i, 0)),
                  pl.BlockSpec((d, n), lambda i: (0, 0), pipeline_mode=pl.Buffered(1))],
        out_specs=[pl.BlockSpec((tm, n), lambda i: (i, 0)), pl.BlockSpec((tm, d), lambda i: (i, 0))],
        out_shape=[jax.ShapeDtypeStruct((m, n), F32), jax.ShapeDtypeStruct((m, d), BF16)],
        compiler_params=_params(1),
        name="input_projection_tail",
    )(x2, w_tail)


def _proj_body(a_ref, b_ref, o_ref):
    o_ref[...] = _dot(a_ref[...], b_ref[...]).astype(o_ref.dtype)


def _resid_body(a_ref, b_ref, x_ref, o_ref, *, alpha):
    o_ref[...] = alpha * x_ref[...] + _dot(a_ref[...], b_ref[...])


def _merge_body(osb_ref, odn_ref, om_ref, wsb_ref, wdn_ref, wm_ref, g0_ref, g1_ref, g2_ref, o_ref):
    y = g0_ref[...].astype(F32) * _dot(osb_ref[...], wsb_ref[...])
    y += g1_ref[...].astype(F32) * _dot(odn_ref[...], wdn_ref[...])
    y += g2_ref[...].astype(F32) * _dot(om_ref[...], wm_ref[...])
    o_ref[...] = y.astype(o_ref.dtype)


def _merge(o_sb, o_dn, o_m, w_sb, w_dn, w_m, gates, tm, tn):
    m = o_sb.shape[0]
    d = w_sb.shape[1]
    row = lambda w: pl.BlockSpec((tm, w), lambda j, i: (i, 0))
    colw = lambda w: pl.BlockSpec((w, tn), lambda j, i: (0, j))
    gate = pl.BlockSpec((tm, tn), lambda j, i: (i, j))
    return pl.pallas_call(
        _merge_body,
        grid=(d // tn, m // tm),
        in_specs=[row(o_sb.shape[1]), row(o_dn.shape[1]), row(o_m.shape[1]),
                  colw(w_sb.shape[0]), colw(w_dn.shape[0]), colw(w_m.shape[0]),
                  gate, gate, gate],
        out_specs=pl.BlockSpec((tm, tn), lambda j, i: (i, j)),
        out_shape=jax.ShapeDtypeStruct((m, d), BF16),
        compiler_params=_params(2),
        name="branch_merge",
    )(o_sb, o_dn, o_m, w_sb, w_dn, w_m, *gates)


def _layernorm_body(r_ref, g_ref, b_ref, o_ref):
    r = r_ref[...]
    mu = jnp.mean(r, axis=-1, keepdims=True)
    rc = r - mu
    var = jnp.mean(rc * rc, axis=-1, keepdims=True)
    o_ref[...] = rc * lax.rsqrt(var + LN_EPS) * g_ref[...] + b_ref[...]


def _layernorm(r, g, b, tm):
    m, d = r.shape
    return pl.pallas_call(
        _layernorm_body,
        grid=(m // tm,),
        in_specs=[pl.BlockSpec((tm, d), lambda i: (i, 0)),
                  pl.BlockSpec((1, d), lambda i: (0, 0)),
                  pl.BlockSpec((1, d), lambda i: (0, 0))],
        out_specs=pl.BlockSpec((tm, d), lambda i: (i, 0)),
        out_shape=jax.ShapeDtypeStruct((m, d), F32),
        compiler_params=_params(1),
        name="post_layernorm",
    )(r, g.reshape(1, d), b.reshape(1, d))


def _sb_body(q_ref, k_ref, v_ref, z_ref, tri_ref, o_ref, vt_ref, acc_ref, lsig_ref, sp_ref, *, tk, nq, hps):
    qi = pl.program_id(2)
    heads = range(hps)
    lanes = lambda hh: slice(hh * HEAD_DIM, (hh + 1) * HEAD_DIM)

    @pl.when(qi == 0)
    def _():
        def transpose_block(c, _):
            blk = v_ref[pl.ds(pl.multiple_of(c * tk, tk), tk), :].astype(F32)
            for hh in heads:
                vt_ref[hh, c] = blk[:, lanes(hh)].T.astype(BF16)
            return 0
        lax.fori_loop(0, nq, transpose_block, 0)

    qs = [q_ref[:, lanes(hh)] for hh in heads]
    key_pos = lax.broadcasted_iota(jnp.int32, (tk, tk), 0)
    qry_pos = lax.broadcasted_iota(jnp.int32, (tk, tk), 1)
    before = key_pos < qry_pos

    def score_dots(j):
        kb = k_ref[pl.ds(pl.multiple_of(j * tk, tk), tk), :]
        return [_dot_nt(kb[:, lanes(hh)], qs[hh]) for hh in heads]

    def scores_store(zts, slot, diagonal):
        for hh in heads:
            sp = _softplus2(zts[hh])
            lsig = zts[hh] - sp
            if diagonal:
                sp = jnp.where(before, sp, 0.0)
                lsig = jnp.where(before, lsig, MASKED_LOG2_WEIGHT)
            lsig_ref[slot, hh] = lsig
            sp_ref[slot, hh] = sp.astype(BF16)

    def suffix_dots(slot):
        return [_dot(tri_ref[...], sp_ref[slot, hh]) for hh in heads]

    def finish(j, slot, sufs, carries):
        out = []
        for hh in heads:
            w = jnp.exp2(lsig_ref[slot, hh] - sufs[hh][:tk] - carries[hh])
            acc_ref[hh] += _dot(vt_ref[hh, j], w.astype(BF16))
            out.append(carries[hh] + sufs[hh][tk:tk + 1])
        return tuple(out)

    def step(state):
        t, carries, _ = state
        slot = t & 1
        sufs = suffix_dots(slot)
        zts = score_dots(qi - t - 1)
        carries = finish(qi - t, slot, sufs, carries)
        scores_store(zts, 1 - slot, False)
        smallest = functools.reduce(jnp.minimum, carries)
        return t + 1, carries, jnp.min(smallest) < SB_UNDERFLOW_LOG2

    acc_ref[...] = jnp.zeros(acc_ref.shape, F32)
    scores_store(score_dots(qi), 0, True)
    t, carries, live = lax.while_loop(lambda s: jnp.logical_and(s[0] < qi, s[2]), step,
                                      (jnp.int32(0), (jnp.zeros((1, tk), F32),) * hps, jnp.bool_(True)))

    @pl.when(live)
    def _():
        last = t & 1
        finish(qi - t, last, suffix_dots(last), carries)

    for hh in heads:
        z = z_ref[:, lanes(hh)].astype(F32)
        o_ref[:, lanes(hh)] = (acc_ref[hh].T * (z * _sigmoid(z))).astype(o_ref.dtype)


def _sb_attention(h, batch, seq, heads, col0):
    t = _tile(seq, SB_BLOCK)
    nq = seq // t
    hps = SB_HEADS_PER_STEP if heads % SB_HEADS_PER_STEP == 0 and col0 % SB_HEADS_PER_STEP == 0 else 1
    ng = heads // hps
    c0 = col0 // hps
    w = hps * HEAD_DIM
    tri = np.zeros((t + SB_TRI_EXTRA_ROWS, t), np.float32)
    tri[:t] = np.triu(np.ones((t, t), np.float32), 1)
    tri[t] = 1.0
    tri = jnp.asarray(tri, BF16)
    body = functools.partial(_sb_body, tk=t, nq=nq, hps=hps)
    whole_seq = lambda g: pl.BlockSpec((seq, w), lambda b, h_, i: (b, c0 + g * ng + h_), pipeline_mode=pl.Buffered(1))
    return pl.pallas_call(
        body,
        grid=(batch, ng, nq),
        in_specs=[pl.BlockSpec((t, w), lambda b, h_, i: (b * nq + i, c0 + h_)),
                  whole_seq(1), whole_seq(2),
                  pl.BlockSpec((t, w), lambda b, h_, i: (b * nq + i, c0 + 3 * ng + h_)),
                  pl.BlockSpec((t + SB_TRI_EXTRA_ROWS, t), lambda b, h_, i: (0, 0))],
        out_specs=pl.BlockSpec((t, w), lambda b, h_, i: (b * nq + i, h_)),
        out_shape=jax.ShapeDtypeStruct((batch * seq, heads * HEAD_DIM), BF16),
        scratch_shapes=[pltpu.VMEM((hps, nq, HEAD_DIM, t), BF16), pltpu.VMEM((hps, HEAD_DIM, t), F32),
                        pltpu.VMEM((2, hps, t, t), F32), pltpu.VMEM((2, hps, t, t), BF16)],
        compiler_params=_params(3),
        name="stickbreak_attention",
    )(h, h, h, h, tri)


_M_INCL, _M_STRICT, _M_PAIR, _M_LEVEL0 = 0, 1, 2, 3
_N_LEVELS = int(np.log2(DN_CHUNK)) - 1
_M_EYE = _M_LEVEL0 + _N_LEVELS
_M_UPPER = _M_EYE + 1
_N_MASKS = _M_UPPER + 1
_DN_STAGES = 2 * _N_LEVELS + 4
GATE_K_PIECES = 2
assert _DN_STAGES % GATE_K_PIECES == 0


def _dn_masks():
    c = DN_CHUNK
    i = np.arange(c)[:, None]
    j = np.arange(c)[None, :]
    masks = [i >= j, i > j, (i % 2 == 1) & (j == i - 1)]
    b = 4
    while b <= c:
        masks.append((i // b == j // b) & (i % b >= b // 2) & (j % b < b // 2))
        b *= 2
    masks += [i == j, i <= j]
    assert len(masks) == _N_MASKS
    return np.stack(masks).astype(np.float32)


def _split3(x):
    p1 = x.astype(BF16)
    r = x - p1.astype(F32)
    p2 = r.astype(BF16)
    p3 = (r - p2.astype(F32)).astype(BF16)
    return p1, p2, p3


def _dn_body(q_ref, k_ref, v_ref, z_ref, cq_ref, ck_ref, cv_ref, b_ref, a_ref, alog_ref, dtb_ref, nw_ref, msk_ref,
             x_ref, wg0_ref, wg1_ref, wg2_ref, bg0_ref, bg1_ref, bg2_ref,
             o_ref, g0_ref, g1_ref, g2_ref,
             gc_ref, beta_ref, gl_ref, xf_ref, lhs_ref, add_ref, state_ref, acc_ref, *, n_groups):
    c = DN_CHUNK
    rows = DN_GROUP * c
    g = pl.program_id(2)
    wg_refs, bg_refs, gate_refs = (wg0_ref, wg1_ref, wg2_ref), (bg0_ref, bg1_ref, bg2_ref), (g0_ref, g1_ref, g2_ref)
    k_piece = x_ref.shape[1] // GATE_K_PIECES

    @pl.when(g == 0)
    def _():
        gd = -jnp.exp(alog_ref[0]) * _softplus(a_ref[0, 0] + dtb_ref[0])
        upper = msk_ref[_M_UPPER].astype(BF16)
        ones = jnp.ones((c, c), BF16)
        pieces = _split3(gd)
        gc_ref[...] = sum(_dot(p, upper) for p in pieces)
        gl_ref[...] = sum(_dot(p, ones) for p in pieces)
        beta_ref[...] = _sigmoid(b_ref[0, 0])
        state_ref[...] = jnp.zeros(state_ref.shape, F32)

    def gate_piece(i):
        ks = slice(i * k_piece, (i + 1) * k_piece)
        for t in range(3):
            part = _dot(x_ref[:, ks], wg_refs[t][ks, :])
            if i == 0:
                acc_ref[t] = part
            elif i < GATE_K_PIECES - 1:
                acc_ref[t] += part
            else:
                gate_refs[t][...] = _sigmoid(acc_ref[t] + part + bg_refs[t][...]).astype(gate_refs[t].dtype)

    def conv_silu(idx, x_blk_ref, cw_ref, first):
        if first:
            xf_ref[idx, 0:CONV_HALO, :] = jnp.zeros((CONV_HALO, HEAD_DIM), F32)
        else:
            xf_ref[idx, 0:CONV_HALO, :] = xf_ref[idx, rows:rows + CONV_HALO, :]
        xf_ref[idx, CONV_HALO:, :] = x_blk_ref[...].astype(F32)
        cw = cw_ref[...]
        y = None
        for s in range(CONV_K):
            tap = xf_ref[idx, CONV_HALO - s:CONV_HALO - s + rows, :] * cw[CONV_K - 1 - s:CONV_K - s, :]
            y = tap if y is None else y + tap
        return y * _sigmoid(y)

    def l2norm(t):
        return t * lax.rsqrt(jnp.sum(t * t, axis=-1, keepdims=True) + L2_EPS)

    def local_work(first, per_stage):
        stage = [0]

        def stage_done():
            for fn in per_stage[stage[0]]:
                fn()
            stage[0] += 1

        slot = g & 1
        qg = l2norm(conv_silu(0, q_ref, cq_ref, first)) * (HEAD_DIM ** -0.5)
        kg = l2norm(conv_silu(1, k_ref, ck_ref, first))
        vg = conv_silu(2, v_ref, cv_ref, first)
        cs = range(DN_GROUP)
        sl = lambda ci: slice(ci * c, (ci + 1) * c)
        qs, ks, vs = [qg[sl(ci)] for ci in cs], [kg[sl(ci)] for ci in cs], [vg[sl(ci)] for ci in cs]
        gls, gc_cols, beta_cols, decays = [], [], [], []
        for ci in cs:
            chunk = g * DN_GROUP + ci
            gc_row = jnp.broadcast_to(gc_ref[pl.ds(chunk, 1), :], (c, c))
            gc_col = gc_row.T
            incl = msk_ref[_M_INCL]
            gls.append(gl_ref[pl.ds(chunk, 1), :])
            gc_cols.append(gc_col)
            beta_cols.append(jnp.broadcast_to(beta_ref[pl.ds(chunk, 1), :], (c, c)).T)
            decays.append(jnp.exp((gc_col - gc_row) * incl) * incl)
        kbs = [ks[ci] * beta_cols[ci] for ci in cs]
        prods = [_dot_nt(jnp.concatenate([qs[ci], kbs[ci]], axis=0).astype(BF16), ks[ci].astype(BF16)) for ci in cs]
        stage_done()
        a_ins = [prods[ci][:c] * decays[ci] for ci in cs]
        a_los = [prods[ci][c:] * decays[ci] * msk_ref[_M_STRICT] for ci in cs]
        t_invs = [msk_ref[_M_EYE] - a_lo * msk_ref[_M_PAIR] for a_lo in a_los]
        for lvl in range(_N_LEVELS):
            offs = [(a_lo * msk_ref[_M_LEVEL0 + lvl]).astype(BF16) for a_lo in a_los]
            tbs = [t_inv.astype(BF16) for t_inv in t_invs]
            xs = [_dot(tbs[ci], offs[ci]).astype(BF16) for ci in cs]
            stage_done()
            t_invs = [t_invs[ci] - _dot(xs[ci], tbs[ci]) for ci in cs]
            stage_done()
        eg_cols = [jnp.exp(gc_col) for gc_col in gc_cols]
        rhss = [jnp.concatenate([kbs[ci] * eg_cols[ci], vs[ci] * beta_cols[ci]], axis=1).astype(BF16) for ci in cs]
        wus = [_dot(t_invs[ci].astype(BF16), rhss[ci]).astype(BF16) for ci in cs]
        stage_done()
        k_ends = [ks[ci] * jnp.exp(gls[ci] - gc_cols[ci]) for ci in cs]
        pns = [_dot(k_ends[ci].T.astype(BF16), wus[ci]) for ci in cs]
        stage_done()
        aw_aus = [_dot(a_ins[ci].astype(BF16), wus[ci]) for ci in cs]
        stage_done()
        assert stage[0] == _DN_STAGES == len(per_stage)
        for ci in cs:
            q_eff = qs[ci] * eg_cols[ci] - aw_aus[ci][:, :HEAD_DIM]
            lhs_ref[slot, ci] = jnp.concatenate([q_eff, pns[ci][:, :HEAD_DIM]], axis=0).astype(BF16)
            add_ref[slot, ci] = jnp.concatenate([aw_aus[ci][:, HEAD_DIM:], pns[ci][:, HEAD_DIM:]], axis=0)

    def scan_chunk(slot, ci, state):
        both = _dot(lhs_ref[slot, ci], state.astype(BF16))
        o = both[:c] + add_ref[slot, ci, :c]
        gl = gl_ref[pl.ds((g - 1) * DN_GROUP + ci, 1), :]
        state = state * jnp.exp(gl) - both[c:] + add_ref[slot, ci, c:]
        o = o * lax.rsqrt(jnp.mean(o * o, axis=-1, keepdims=True) + RMS_EPS) * nw_ref[...]
        zr = z_ref[ci * c:(ci + 1) * c, :].astype(F32)
        o_ref[ci * c:(ci + 1) * c, :] = (o * (zr * _sigmoid(zr))).astype(o_ref.dtype)
        return state

    def run(first, with_scan):
        box = [state_ref[...]] if with_scan else None

        def scan_hook(ci):
            box[0] = scan_chunk(1 - (g & 1), ci, box[0])

        per_stage = [[] for _ in range(_DN_STAGES)]
        for i in range(GATE_K_PIECES):
            per_stage[i * (_DN_STAGES // GATE_K_PIECES)].append(functools.partial(gate_piece, i))
        if with_scan:
            for ci in range(DN_GROUP):
                per_stage[2 * ci + 1].append(functools.partial(scan_hook, ci))
        local_work(first, per_stage)
        if with_scan:
            state_ref[...] = box[0]

    @pl.when(g == 0)
    def _():
        run(True, False)

    @pl.when(jnp.logical_and(g > 0, g < n_groups))
    def _():
        run(False, True)

    @pl.when(g == n_groups)
    def _():
        state = state_ref[...]
        for ci in range(DN_GROUP):
            state = scan_chunk((n_groups - 1) & 1, ci, state)


def _deltanet_and_gates(h, conv_w, ba, a_log, dt_bias, norm_w, xb, w_gate, b_gate, batch, seq, heads, col0):
    c = DN_CHUNK
    rows = DN_GROUP * c
    assert seq % rows == 0 and w_gate.shape[0] == 3
    n_chunks, ng = seq // c, seq // rows
    m, d = xb.shape
    gw = d // heads
    assert gw * heads == d and gw % HEAD_DIM == 0 and d % GATE_K_PIECES == 0
    masks = jnp.asarray(_dn_masks())
    lane = lambda v: jnp.broadcast_to(v.astype(F32)[:, None, None], (heads, 1, c))
    body = functools.partial(_dn_body, n_groups=ng)
    cur = lambda b, g: b * ng + jnp.minimum(g, ng - 1)
    prev = lambda b, g: b * ng + jnp.maximum(g - 1, 0)
    hcol = lambda t: pl.BlockSpec((rows, HEAD_DIM), lambda b, h_, g: (cur(b, g), col0 + t * heads + h_))
    cw = lambda t: pl.BlockSpec((CONV_K, HEAD_DIM), lambda b, h_, g: (0, t * heads + h_))
    wgs = lambda t: pl.BlockSpec((None, d, gw), lambda b, h_, g: (t, 0, h_))
    bgs = lambda t: pl.BlockSpec((None, 1, gw), lambda b, h_, g: (t, 0, h_))
    b_gate = b_gate.reshape(3, 1, d)
    gate_out = pl.BlockSpec((rows, gw), lambda b, h_, g: (cur(b, g), h_))
    outs = pl.pallas_call(
        body,
        grid=(batch, heads, ng + 1),
        in_specs=[hcol(0), hcol(1), hcol(2),
                  pl.BlockSpec((rows, HEAD_DIM), lambda b, h_, g: (prev(b, g), col0 + 3 * heads + h_)),
                  cw(0), cw(1), cw(2),
                  pl.BlockSpec((1, 1, n_chunks, c), lambda b, h_, g: (b, h_, 0, 0)),
                  pl.BlockSpec((1, 1, n_chunks, c), lambda b, h_, g: (b, heads + h_, 0, 0)),
                  pl.BlockSpec((1, 1, c), lambda b, h_, g: (h_, 0, 0)),
                  pl.BlockSpec((1, 1, c), lambda b, h_, g: (h_, 0, 0)),
                  pl.BlockSpec((1, HEAD_DIM), lambda b, h_, g: (0, 0)),
                  pl.BlockSpec((_N_MASKS, c, c), lambda b, h_, g: (0, 0, 0)),
                  pl.BlockSpec((rows, d), lambda b, h_, g: (cur(b, g), 0)),
                  wgs(0), wgs(1), wgs(2), bgs(0), bgs(1), bgs(2)],
        out_specs=[pl.BlockSpec((rows, HEAD_DIM), lambda b, h_, g: (prev(b, g), h_)), gate_out, gate_out, gate_out],
        out_shape=[jax.ShapeDtypeStruct((m, heads * HEAD_DIM), BF16)] + [jax.ShapeDtypeStruct((m, d), BF16)] * 3,
        scratch_shapes=[pltpu.VMEM((n_chunks, c), F32)] * 3
        + [pltpu.VMEM((3, CONV_HALO + rows, HEAD_DIM), F32),
           pltpu.VMEM((2, DN_GROUP, 2 * c, HEAD_DIM), BF16), pltpu.VMEM((2, DN_GROUP, 2 * c, HEAD_DIM), F32),
           pltpu.VMEM((HEAD_DIM, HEAD_DIM), F32), pltpu.VMEM((3, rows, gw), F32)],
        compiler_params=_params(3),
        name="deltanet_and_gates",
    )(h, h, h, h, conv_w, conv_w, conv_w, ba, ba, lane(a_log), lane(dt_bias), norm_w.reshape(1, HEAD_DIM), masks,
      xb, w_gate, w_gate, w_gate, b_gate, b_gate, b_gate)
    return outs[0], outs[1:]


def _mem_body(q_ref, z_ref, kv_ref, o_ref, *, width):
    for hh in range(width // HEAD_DIM):
        sl = slice(hh * HEAD_DIM, (hh + 1) * HEAD_DIM)
        q = (q_ref[:, sl] * (HEAD_DIM ** -0.5)).astype(BF16)
        k = kv_ref[:, sl]
        v = kv_ref[:, width + hh * HEAD_DIM:width + (hh + 1) * HEAD_DIM]
        s = _dot_nt(q, k)
        p = jnp.exp(s - jnp.max(s, axis=-1, keepdims=True))
        o = _dot(p.astype(BF16), v) * (1.0 / jnp.sum(p, axis=-1, keepdims=True))
        z = z_ref[:, sl]
        o_ref[:, sl] = (o * (z * _sigmoid(z))).astype(o_ref.dtype)


def _mem_attention(h_tail, kv, batch, seq, n_mem, width, tq):
    nq = seq // tq
    body = functools.partial(_mem_body, width=width)
    return pl.pallas_call(
        body,
        grid=(batch, nq),
        in_specs=[pl.BlockSpec((tq, width), lambda b, i: (b * nq + i, 0)),
                  pl.BlockSpec((tq, width), lambda b, i: (b * nq + i, 1)),
                  pl.BlockSpec((n_mem, 2 * width), lambda b, i: (b, 0))],
        out_specs=pl.BlockSpec((tq, width), lambda b, i: (b * nq + i, 0)),
        out_shape=jax.ShapeDtypeStruct((batch * seq, width), BF16),
        compiler_params=_params(2),
        name="memory_attention",
    )(h_tail, h_tail, kv)


def _layer_tiles(m, d, seq, n_main, mem_rows, mem_cols):
    return dict(rows=_tile(m, 1024), in_cols=_tile(n_main, 1024), out_cols=_tile(d, 512), tail_rows=_tile(m, 512),
                mem_rows=_tile(mem_rows, 512), mem_cols=_tile(mem_cols, 1024), mem_queries=_tile(seq, 512),
                norm_rows=_tile(m, 256))


def _layer(x, mem, w_in, conv_w, a_log, dt_bias, dn_norm_w, w_mem_kv, w_gate, b_gate,
           w_up_sb, w_up_dn, w_up_mem, w_out, ln_g, ln_b, alpha):
    batch, seq, d = x.shape
    n_mem = mem.shape[1]
    m = batch * seq
    sbw, dnw, memw = w_up_sb.shape[0], w_up_dn.shape[0], w_up_mem.shape[0]
    sbh, dnh = sbw // HEAD_DIM, dnw // HEAD_DIM
    n_main = 4 * sbw + 4 * dnw
    ba0 = n_main
    qm0 = ba0 + 2 * dnh
    assert w_in.shape[1] == qm0 + 2 * memw

    x2 = x.reshape(m, d)
    tiles = _layer_tiles(m, d, seq, n_main, batch * n_mem, 2 * memw)
    tm = tiles["rows"]

    w_in_b = w_in.astype(BF16)
    pad = (-2 * dnh) % HEAD_DIM
    w_tail = jnp.concatenate([w_in_b[:, qm0:], w_in_b[:, ba0:qm0], jnp.zeros((d, pad), BF16)], axis=1)
    h_tail, xb = _tail_projection(x2, w_tail, tiles["tail_rows"])
    col_scale = jnp.concatenate([jnp.full((sbw,), HEAD_DIM ** -0.5 * LOG2E, F32), jnp.ones((n_main - sbw,), F32)])
    h = _tiled_matmul(_proj_scale_body, xb, w_in_b,
                      [(col_scale.reshape(1, n_main), (1, tiles["in_cols"]), lambda j, i: (0, j))],
                      BF16, tm, tiles["in_cols"], "input_projection", n_cols=n_main)

    o_sb = _sb_attention(h, batch, seq, sbh, 0)

    ba = h_tail[:, 2 * memw:2 * memw + 2 * dnh].reshape(batch, seq, 2 * dnh)
    ba = ba.transpose(0, 2, 1).reshape(batch, 2 * dnh, seq // DN_CHUNK, DN_CHUNK)
    o_dn, gates = _deltanet_and_gates(h, conv_w, ba, a_log, dt_bias, dn_norm_w, xb, w_gate.astype(BF16), b_gate,
                                      batch, seq, dnh, 4 * sbh)

    kv = _tiled_matmul(_proj_body, mem.reshape(batch * n_mem, d).astype(BF16), w_mem_kv.astype(BF16), [], BF16,
                       tiles["mem_rows"], tiles["mem_cols"], "memory_kv_projection")
    o_m = _mem_attention(h_tail, kv, batch, seq, n_mem, memw, tiles["mem_queries"])

    y = _merge(o_sb, o_dn, o_m, w_up_sb.astype(BF16), w_up_dn.astype(BF16), w_up_mem.astype(BF16), gates, tm,
               tiles["out_cols"])
    r = _tiled_matmul(functools.partial(_resid_body, alpha=alpha), y, w_out.astype(BF16),
                      [(x2, (tm, tiles["out_cols"]), lambda j, i: (i, j))], F32, tm, tiles["out_cols"],
                      "output_projection")
    return _layernorm(r, ln_g, ln_b, tiles["norm_rows"]).reshape(batch, seq, d)


def kernel(x, mem, w_in, conv_w, a_log, dt_bias, dn_norm_w, w_mem_kv, w_gate, b_gate, w_up_sb, w_up_dn, w_up_mem,
           w_out, ln_g, ln_b):
    depth = w_in.shape[0]
    alpha = (2.0 * depth) ** 0.25
    for l in range(depth):
        x = _layer(x, mem, w_in[l], conv_w[l], a_log[l], dt_bias[l], dn_norm_w[l], w_mem_kv[l], w_gate[l],
                   b_gate[l], w_up_sb[l], w_up_dn[l], w_up_mem[l], w_out[l], ln_g[l], ln_b[l], alpha)
    return x
```

```python
import functools

import numpy as np
import jax
import jax.numpy as jnp
from jax import lax
from jax.experimental import pallas as pl
from jax.experimental.pallas import tpu as pltpu

HEAD_DIM = 128
CONV_K = 4
CONV_HALO = 8
LN_EPS = 1e-5
RMS_EPS = 1e-6
L2_EPS = 1e-6
LOG2E = 1.4426950408889634
SOFTPLUS2_CLAMP = 64.0
MASKED_LOG2_WEIGHT = -1e30
SB_TRI_EXTRA_ROWS = 16
SB_UNDERFLOW_LOG2 = 1100.0

DN_CHUNK = 128
DN_GROUP = 8
SB_BLOCK = 256
SB_HEADS_PER_STEP = 4
V7X_VMEM_LIMIT_BYTES = 48 * 1024 * 1024

F32 = jnp.float32
BF16 = jnp.bfloat16


def _params(n_axes):
    return pltpu.CompilerParams(dimension_semantics=("arbitrary",) * n_axes,
                                vmem_limit_bytes=V7X_VMEM_LIMIT_BYTES)


def _dot(a, b):
    return jnp.dot(a, b, preferred_element_type=F32)


def _dot_nt(a, b):
    return lax.dot_general(a, b, (((1,), (1,)), ((), ())), preferred_element_type=F32)


def _sigmoid(x):
    return 1.0 / (1.0 + jnp.exp(-x))


def _softplus(x):
    return jnp.maximum(x, 0.0) + jnp.log(1.0 + jnp.exp(-jnp.abs(x)))


def _softplus2(x):
    return jnp.maximum(x, jnp.log(1.0 + jnp.exp2(jnp.minimum(x, SOFTPLUS2_CLAMP))) * LOG2E)


def _tile(n, pref):
    t = min(n, pref)
    assert n % t == 0, (n, t)
    return t


def _tiled_matmul(body, a, b, extra, out_dtype, tm, tn, name, n_cols=None):
    m, k = a.shape
    n = b.shape[1] if n_cols is None else n_cols
    assert m % tm == 0 and n % tn == 0
    in_specs = [pl.BlockSpec((tm, k), lambda j, i: (i, 0)), pl.BlockSpec((k, tn), lambda j, i: (0, j))]
    in_specs += [pl.BlockSpec(blk, imap) for _, blk, imap in extra]
    return pl.pallas_call(
        body,
        grid=(n // tn, m // tm),
        in_specs=in_specs,
        out_specs=pl.BlockSpec((tm, tn), lambda j, i: (i, j)),
        out_shape=jax.ShapeDtypeStruct((m, n), out_dtype),
        compiler_params=_params(2),
        name=name,
    )(a, b, *[e[0] for e in extra])


def _cast_body(x_ref, o_ref):
    o_ref[...] = x_ref[...].astype(o_ref.dtype)


def _to_bf16(w, rows):
    r, c = w.shape
    return pl.pallas_call(
        _cast_body,
        grid=(r // rows,),
        in_specs=[pl.BlockSpec((rows, c), lambda i: (i, 0))],
        out_specs=pl.BlockSpec((rows, c), lambda i: (i, 0)),
        out_shape=jax.ShapeDtypeStruct((r, c), BF16),
        compiler_params=_params(1),
        name="weight_cast",
    )(w)


def _proj_scale_body(a_ref, b_ref, s_ref, o_ref):
    o_ref[...] = (_dot(a_ref[...], b_ref[...]) * s_ref[...]).astype(o_ref.dtype)


def _tail_body(x_ref, w_ref, o_ref, xb_ref):
    xb = x_ref[...].astype(BF16)
    xb_ref[...] = xb
    o_ref[...] = _dot(xb, w_ref[...])


def _tail_projection(x2, w_tail, tm):
    m, d = x2.shape
    n = w_tail.shape[1]
    return pl.pallas_call(
        _tail_body,
        grid=(m // tm,),
        in_specs=[pl.BlockSpec((tm, d), lambda i: (i, 0)),
                  pl.BlockSpec((d, n), lambda i: (0, 0), pipeline_mode=pl.Buffered(1))],
        out_specs=[pl.BlockSpec((tm, n), lambda i: (i, 0)), pl.BlockSpec((tm, d), lambda i: (i, 0))],
        out_shape=[jax.ShapeDtypeStruct((m, n), F32), jax.ShapeDtypeStruct((m, d), BF16)],
        compiler_params=_params(1),
        name="input_projection_tail",
    )(x2, w_tail)


def _proj_body(a_ref, b_ref, o_ref):
    o_ref[...] = _dot(a_ref[...], b_ref[...]).astype(o_ref.dtype)


def _resid_body(a_ref, b_ref, x_ref, o_ref, *, alpha):
    o_ref[...] = alpha * x_ref[...] + _dot(a_ref[...], b_ref[...])


def _merge_body(osb_ref, odn_ref, om_ref, wsb_ref, wdn_ref, wm_ref, g0_ref, g1_ref, g2_ref, o_ref):
    y = g0_ref[...].astype(F32) * _dot(osb_ref[...], wsb_ref[...])
    y += g1_ref[...].astype(F32) * _dot(odn_ref[...], wdn_ref[...])
    y += g2_ref[...].astype(F32) * _dot(om_ref[...], wm_ref[...])
    o_ref[...] = y.astype(o_ref.dtype)


def _merge(o_sb, o_dn, o_m, w_sb, w_dn, w_m, gates, tm, tn):
    m = o_sb.shape[0]
    d = w_sb.shape[1]
    row = lambda w: pl.BlockSpec((tm, w), lambda j, i: (i, 0))
    colw = lambda w: pl.BlockSpec((w, tn), lambda j, i: (0, j))
    gate = pl.BlockSpec((tm, tn), lambda j, i: (i, j))
    return pl.pallas_call(
        _merge_body,
        grid=(d // tn, m // tm),
        in_specs=[row(o_sb.shape[1]), row(o_dn.shape[1]), row(o_m.shape[1]),
                  colw(w_sb.shape[0]), colw(w_dn.shape[0]), colw(w_m.shape[0]),
                  gate, gate, gate],
        out_specs=pl.BlockSpec((tm, tn), lambda j, i: (i, j)),
        out_shape=jax.ShapeDtypeStruct((m, d), BF16),
        compiler_params=_params(2),
        name="branch_merge",
    )(o_sb, o_dn, o_m, w_sb, w_dn, w_m, *gates)


def _layernorm_body(r_ref, g_ref, b_ref, o_ref):
    r = r_ref[...]
    mu = jnp.mean(r, axis=-1, keepdims=True)
    rc = r - mu
    var = jnp.mean(rc * rc, axis=-1, keepdims=True)
    o_ref[...] = rc * lax.rsqrt(var + LN_EPS) * g_ref[...] + b_ref[...]


def _layernorm(r, g, b, tm):
    m, d = r.shape
    return pl.pallas_call(
        _layernorm_body,
        grid=(m // tm,),
        in_specs=[pl.BlockSpec((tm, d), lambda i: (i, 0)),
                  pl.BlockSpec((1, d), lambda i: (0, 0)),
                  pl.BlockSpec((1, d), lambda i: (0, 0))],
        out_specs=pl.BlockSpec((tm, d), lambda i: (i, 0)),
        out_shape=jax.ShapeDtypeStruct((m, d), F32),
        compiler_params=_params(1),
        name="post_layernorm",
    )(r, g.reshape(1, d), b.reshape(1, d))


def _sb_body(q_ref, k_ref, v_ref, z_ref, tri_ref, o_ref, vt_ref, acc_ref, lsig_ref, sp_ref, *, tk, nq, hps):
    qi = pl.program_id(2)
    heads = range(hps)
    lanes = lambda hh: slice(hh * HEAD_DIM, (hh + 1) * HEAD_DIM)

    @pl.when(qi == 0)
    def _():
        def transpose_block(c, _):
            blk = v_ref[pl.ds(pl.multiple_of(c * tk, tk), tk), :].astype(F32)
            for hh in heads:
                vt_ref[hh, c] = blk[:, lanes(hh)].T.astype(BF16)
            return 0
        lax.fori_loop(0, nq, transpose_block, 0)

    qs = [q_ref[:, lanes(hh)] for hh in heads]
    key_pos = lax.broadcasted_iota(jnp.int32, (tk, tk), 0)
    qry_pos = lax.broadcasted_iota(jnp.int32, (tk, tk), 1)
    before = key_pos < qry_pos

    def score_dots(j):
        kb = k_ref[pl.ds(pl.multiple_of(j * tk, tk), tk), :]
        return [_dot_nt(kb[:, lanes(hh)], qs[hh]) for hh in heads]

    def scores_store(zts, slot, diagonal):
        for hh in heads:
            sp = _softplus2(zts[hh])
            lsig = zts[hh] - sp
            if diagonal:
                sp = jnp.where(before, sp, 0.0)
                lsig = jnp.where(before, lsig, MASKED_LOG2_WEIGHT)
            lsig_ref[slot, hh] = lsig
            sp_ref[slot, hh] = sp.astype(BF16)

    def suffix_dots(slot):
        return [_dot(tri_ref[...], sp_ref[slot, hh]) for hh in heads]

    def finish(j, slot, sufs, carries):
        out = []
        for hh in heads:
            w = jnp.exp2(lsig_ref[slot, hh] - sufs[hh][:tk] - carries[hh])
            acc_ref[hh] += _dot(vt_ref[hh, j], w.astype(BF16))
            out.append(carries[hh] + sufs[hh][tk:tk + 1])
        return tuple(out)

    def step(state):
        t, carries, _ = state
        slot = t & 1
        sufs = suffix_dots(slot)
        zts = score_dots(qi - t - 1)
        carries = finish(qi - t, slot, sufs, carries)
        scores_store(zts, 1 - slot, False)
        smallest = functools.reduce(jnp.minimum, carries)
        return t + 1, carries, jnp.min(smallest) < SB_UNDERFLOW_LOG2

    acc_ref[...] = jnp.zeros(acc_ref.shape, F32)
    scores_store(score_dots(qi), 0, True)
    t, carries, live = lax.while_loop(lambda s: jnp.logical_and(s[0] < qi, s[2]), step,
                                      (jnp.int32(0), (jnp.zeros((1, tk), F32),) * hps, jnp.bool_(True)))

    @pl.when(live)
    def _():
        last = t & 1
        finish(qi - t, last, suffix_dots(last), carries)

    for hh in heads:
        z = z_ref[:, lanes(hh)].astype(F32)
        o_ref[:, lanes(hh)] = (acc_ref[hh].T * (z * _sigmoid(z))).astype(o_ref.dtype)


def _sb_attention(h, batch, seq, heads, col0):
    t = _tile(seq, SB_BLOCK)
    nq = seq // t
    hps = SB_HEADS_PER_STEP if heads % SB_HEADS_PER_STEP == 0 and col0 % SB_HEADS_PER_STEP == 0 else 1
    ng = heads // hps
    c0 = col0 // hps
    w = hps * HEAD_DIM
    tri = np.zeros((t + SB_TRI_EXTRA_ROWS, t), np.float32)
    tri[:t] = np.triu(np.ones((t, t), np.float32), 1)
    tri[t] = 1.0
    tri = jnp.asarray(tri, BF16)
    body = functools.partial(_sb_body, tk=t, nq=nq, hps=hps)
    whole_seq = lambda g: pl.BlockSpec((seq, w), lambda b, h_, i: (b, c0 + g * ng + h_), pipeline_mode=pl.Buffered(1))
    return pl.pallas_call(
        body,
        grid=(batch, ng, nq),
        in_specs=[pl.BlockSpec((t, w), lambda b, h_, i: (b * nq + i, c0 + h_)),
                  whole_seq(1), whole_seq(2),
                  pl.BlockSpec((t, w), lambda b, h_, i: (b * nq + i, c0 + 3 * ng + h_)),
                  pl.BlockSpec((t + SB_TRI_EXTRA_ROWS, t), lambda b, h_, i: (0, 0))],
        out_specs=pl.BlockSpec((t, w), lambda b, h_, i: (b * nq + i, h_)),
        out_shape=jax.ShapeDtypeStruct((batch * seq, heads * HEAD_DIM), BF16),
        scratch_shapes=[pltpu.VMEM((hps, nq, HEAD_DIM, t), BF16), pltpu.VMEM((hps, HEAD_DIM, t), F32),
                        pltpu.VMEM((2, hps, t, t), F32), pltpu.VMEM((2, hps, t, t), BF16)],
        compiler_params=_params(3),
        name="stickbreak_attention",
    )(h, h, h, h, tri)


_M_INCL, _M_STRICT, _M_PAIR, _M_LEVEL0 = 0, 1, 2, 3
_N_LEVELS = int(np.log2(DN_CHUNK)) - 1
_M_EYE = _M_LEVEL0 + _N_LEVELS
_M_UPPER = _M_EYE + 1
_N_MASKS = _M_UPPER + 1
_DN_STAGES = 2 * _N_LEVELS + 4
GATE_K_PIECES = 4
assert _DN_STAGES % GATE_K_PIECES == 0


def _dn_masks():
    c = DN_CHUNK
    i = np.arange(c)[:, None]
    j = np.arange(c)[None, :]
    masks = [i >= j, i > j, (i % 2 == 1) & (j == i - 1)]
    b = 4
    while b <= c:
        masks.append((i // b == j // b) & (i % b >= b // 2) & (j % b < b // 2))
        b *= 2
    masks += [i == j, i <= j]
    assert len(masks) == _N_MASKS
    return np.stack(masks).astype(np.float32)


def _split3(x):
    p1 = x.astype(BF16)
    r = x - p1.astype(F32)
    p2 = r.astype(BF16)
    p3 = (r - p2.astype(F32)).astype(BF16)
    return p1, p2, p3


def _dn_body(q_ref, k_ref, v_ref, z_ref, cq_ref, ck_ref, cv_ref, b_ref, a_ref, alog_ref, dtb_ref, nw_ref, msk_ref,
             x_ref, wg0_ref, wg1_ref, wg2_ref, bg0_ref, bg1_ref, bg2_ref,
             o_ref, g0_ref, g1_ref, g2_ref,
             gc_ref, beta_ref, gl_ref, xf_ref, lhs_ref, add_ref, state_ref, acc_ref, *, n_groups):
    c = DN_CHUNK
    rows = DN_GROUP * c
    g = pl.program_id(2)
    wg_refs, bg_refs, gate_refs = (wg0_ref, wg1_ref, wg2_ref), (bg0_ref, bg1_ref, bg2_ref), (g0_ref, g1_ref, g2_ref)
    k_piece = x_ref.shape[1] // GATE_K_PIECES

    @pl.when(g == 0)
    def _():
        gd = -jnp.exp(alog_ref[0]) * _softplus(a_ref[0, 0] + dtb_ref[0])
        upper = msk_ref[_M_UPPER].astype(BF16)
        ones = jnp.ones((c, c), BF16)
        pieces = _split3(gd)
        gc_ref[...] = sum(_dot(p, upper) for p in pieces)
        gl_ref[...] = sum(_dot(p, ones) for p in pieces)
        beta_ref[...] = _sigmoid(b_ref[0, 0])
        state_ref[...] = jnp.zeros(state_ref.shape, F32)

    def gate_piece(i):
        ks = slice(i * k_piece, (i + 1) * k_piece)
        for t in range(3):
            part = _dot(x_ref[:, ks], wg_refs[t][ks, :])
            if i == 0:
                acc_ref[t] = part
            elif i < GATE_K_PIECES - 1:
                acc_ref[t] += part
            else:
                gate_refs[t][...] = _sigmoid(acc_ref[t] + part + bg_refs[t][...]).astype(gate_refs[t].dtype)

    def conv_silu(idx, x_blk_ref, cw_ref, first):
        if first:
            xf_ref[idx, 0:CONV_HALO, :] = jnp.zeros((CONV_HALO, HEAD_DIM), F32)
        else:
            xf_ref[idx, 0:CONV_HALO, :] = xf_ref[idx, rows:rows + CONV_HALO, :]
        xf_ref[idx, CONV_HALO:, :] = x_blk_ref[...].astype(F32)
        cw = cw_ref[...]
        y = None
        for s in range(CONV_K):
            tap = xf_ref[idx, CONV_HALO - s:CONV_HALO - s + rows, :] * cw[CONV_K - 1 - s:CONV_K - s, :]
            y = tap if y is None else y + tap
        return y * _sigmoid(y)

    def l2norm(t):
        return t * lax.rsqrt(jnp.sum(t * t, axis=-1, keepdims=True) + L2_EPS)

    def local_work(first, per_stage):
        stage = [0]

        def stage_done():
            for fn in per_stage[stage[0]]:
                fn()
            stage[0] += 1

        slot = g & 1
        qg = l2norm(conv_silu(0, q_ref, cq_ref, first)) * (HEAD_DIM ** -0.5)
        kg = l2norm(conv_silu(1, k_ref, ck_ref, first))
        vg = conv_silu(2, v_ref, cv_ref, first)
        cs = range(DN_GROUP)
        sl = lambda ci: slice(ci * c, (ci + 1) * c)
        qs, ks, vs = [qg[sl(ci)] for ci in cs], [kg[sl(ci)] for ci in cs], [vg[sl(ci)] for ci in cs]
        gls, gc_cols, beta_cols, decays = [], [], [], []
        for ci in cs:
            chunk = g * DN_GROUP + ci
            gc_row = jnp.broadcast_to(gc_ref[pl.ds(chunk, 1), :], (c, c))
            gc_col = gc_row.T
            incl = msk_ref[_M_INCL]
            gls.append(gl_ref[pl.ds(chunk, 1), :])
            gc_cols.append(gc_col)
            beta_cols.append(jnp.broadcast_to(beta_ref[pl.ds(chunk, 1), :], (c, c)).T)
            decays.append(jnp.exp((gc_col - gc_row) * incl) * incl)
        kbs = [ks[ci] * beta_cols[ci] for ci in cs]
        prods = [_dot_nt(jnp.concatenate([qs[ci], kbs[ci]], axis=0).astype(BF16), ks[ci].astype(BF16)) for ci in cs]
        stage_done()
        a_ins = [prods[ci][:c] * decays[ci] for ci in cs]
        a_los = [prods[ci][c:] * decays[ci] * msk_ref[_M_STRICT] for ci in cs]
        t_invs = [msk_ref[_M_EYE] - a_lo * msk_ref[_M_PAIR] for a_lo in a_los]
        for lvl in range(_N_LEVELS):
            offs = [(a_lo * msk_ref[_M_LEVEL0 + lvl]).astype(BF16) for a_lo in a_los]
            tbs = [t_inv.astype(BF16) for t_inv in t_invs]
            xs = [_dot(tbs[ci], offs[ci]).astype(BF16) for ci in cs]
            stage_done()
            t_invs = [t_invs[ci] - _dot(xs[ci], tbs[ci]) for ci in cs]
            stage_done()
        eg_cols = [jnp.exp(gc_col) for gc_col in gc_cols]
        rhss = [jnp.concatenate([kbs[ci] * eg_cols[ci], vs[ci] * beta_cols[ci]], axis=1).astype(BF16) for ci in cs]
        wus = [_dot(t_invs[ci].astype(BF16), rhss[ci]).astype(BF16) for ci in cs]
        stage_done()
        k_ends = [ks[ci] * jnp.exp(gls[ci] - gc_cols[ci]) for ci in cs]
        pns = [_dot(k_ends[ci].T.astype(BF16), wus[ci]) for ci in cs]
        stage_done()
        aw_aus = [_dot(a_ins[ci].astype(BF16), wus[ci]) for ci in cs]
        stage_done()
        assert stage[0] == _DN_STAGES == len(per_stage)
        for ci in cs:
            q_eff = qs[ci] * eg_cols[ci] - aw_aus[ci][:, :HEAD_DIM]
            lhs_ref[slot, ci] = jnp.concatenate([q_eff, pns[ci][:, :HEAD_DIM]], axis=0).astype(BF16)
            add_ref[slot, ci] = jnp.concatenate([aw_aus[ci][:, HEAD_DIM:], pns[ci][:, HEAD_DIM:]], axis=0)

    def scan_chunk(slot, ci, state):
        both = _dot(lhs_ref[slot, ci], state.astype(BF16))
        o = both[:c] + add_ref[slot, ci, :c]
        gl = gl_ref[pl.ds((g - 1) * DN_GROUP + ci, 1), :]
        state = state * jnp.exp(gl) - both[c:] + add_ref[slot, ci, c:]
        o = o * lax.rsqrt(jnp.mean(o * o, axis=-1, keepdims=True) + RMS_EPS) * nw_ref[...]
        zr = z_ref[ci * c:(ci + 1) * c, :].astype(F32)
        o_ref[ci * c:(ci + 1) * c, :] = (o * (zr * _sigmoid(zr))).astype(o_ref.dtype)
        return state

    def run(first, with_scan):
        box = [state_ref[...]] if with_scan else None

        def scan_hook(ci):
            box[0] = scan_chunk(1 - (g & 1), ci, box[0])

        per_stage = [[] for _ in range(_DN_STAGES)]
        for i in range(GATE_K_PIECES):
            per_stage[i * (_DN_STAGES // GATE_K_PIECES)].append(functools.partial(gate_piece, i))
        if with_scan:
            for ci in range(DN_GROUP):
                per_stage[2 * ci + 1].append(functools.partial(scan_hook, ci))
        local_work(first, per_stage)
        if with_scan:
            state_ref[...] = box[0]

    @pl.when(g == 0)
    def _():
        run(True, False)

    @pl.when(jnp.logical_and(g > 0, g < n_groups))
    def _():
        run(False, True)

    @pl.when(g == n_groups)
    def _():
        state = state_ref[...]
        for ci in range(DN_GROUP):
            state = scan_chunk((n_groups - 1) & 1, ci, state)


def _deltanet_and_gates(h, conv_w, ba, a_log, dt_bias, norm_w, xb, w_gate, b_gate, batch, seq, heads, col0):
    c = DN_CHUNK
    rows = DN_GROUP * c
    assert seq % rows == 0 and w_gate.shape[0] == 3
    n_chunks, ng = seq // c, seq // rows
    m, d = xb.shape
    gw = d // heads
    assert gw * heads == d and gw % HEAD_DIM == 0 and d % GATE_K_PIECES == 0
    masks = jnp.asarray(_dn_masks())
    lane = lambda v: jnp.broadcast_to(v.astype(F32)[:, None, None], (heads, 1, c))
    body = functools.partial(_dn_body, n_groups=ng)
    cur = lambda b, g: b * ng + jnp.minimum(g, ng - 1)
    prev = lambda b, g: b * ng + jnp.maximum(g - 1, 0)
    hcol = lambda t: pl.BlockSpec((rows, HEAD_DIM), lambda b, h_, g: (cur(b, g), col0 + t * heads + h_))
    cw = lambda t: pl.BlockSpec((CONV_K, HEAD_DIM), lambda b, h_, g: (0, t * heads + h_))
    wgs = lambda t: pl.BlockSpec((None, d, gw), lambda b, h_, g: (t, 0, h_))
    bgs = lambda t: pl.BlockSpec((None, 1, gw), lambda b, h_, g: (t, 0, h_))
    b_gate = b_gate.reshape(3, 1, d)
    gate_out = pl.BlockSpec((rows, gw), lambda b, h_, g: (cur(b, g), h_))
    outs = pl.pallas_call(
        body,
        grid=(batch, heads, ng + 1),
        in_specs=[hcol(0), hcol(1), hcol(2),
                  pl.BlockSpec((rows, HEAD_DIM), lambda b, h_, g: (prev(b, g), col0 + 3 * heads + h_)),
                  cw(0), cw(1), cw(2),
                  pl.BlockSpec((1, 1, n_chunks, c), lambda b, h_, g: (b, h_, 0, 0)),
                  pl.BlockSpec((1, 1, n_chunks, c), lambda b, h_, g: (b, heads + h_, 0, 0)),
                  pl.BlockSpec((1, 1, c), lambda b, h_, g: (h_, 0, 0)),
                  pl.BlockSpec((1, 1, c), lambda b, h_, g: (h_, 0, 0)),
                  pl.BlockSpec((1, HEAD_DIM), lambda b, h_, g: (0, 0)),
                  pl.BlockSpec((_N_MASKS, c, c), lambda b, h_, g: (0, 0, 0)),
                  pl.BlockSpec((rows, d), lambda b, h_, g: (cur(b, g), 0)),
                  wgs(0), wgs(1), wgs(2), bgs(0), bgs(1), bgs(2)],
        out_specs=[pl.BlockSpec((rows, HEAD_DIM), lambda b, h_, g: (prev(b, g), h_)), gate_out, gate_out, gate_out],
        out_shape=[jax.ShapeDtypeStruct((m, heads * HEAD_DIM), BF16)] + [jax.ShapeDtypeStruct((m, d), BF16)] * 3,
        scratch_shapes=[pltpu.VMEM((n_chunks, c), F32)] * 3
        + [pltpu.VMEM((3, CONV_HALO + rows, HEAD_DIM), F32),
           pltpu.VMEM((2, DN_GROUP, 2 * c, HEAD_DIM), BF16), pltpu.VMEM((2, DN_GROUP, 2 * c, HEAD_DIM), F32),
           pltpu.VMEM((HEAD_DIM, HEAD_DIM), F32), pltpu.VMEM((3, rows, gw), F32)],
        compiler_params=_params(3),
        name="deltanet_and_gates",
    )(h, h, h, h, conv_w, conv_w, conv_w, ba, ba, lane(a_log), lane(dt_bias), norm_w.reshape(1, HEAD_DIM), masks,
      xb, w_gate, w_gate, w_gate, b_gate, b_gate, b_gate)
    return outs[0], outs[1:]


def _mem_body(q_ref, z_ref, kv_ref, o_ref, *, width):
    for hh in range(width // HEAD_DIM):
        sl = slice(hh * HEAD_DIM, (hh + 1) * HEAD_DIM)
        q = (q_ref[:, sl] * (HEAD_DIM ** -0.5)).astype(BF16)
        k = kv_ref[:, sl]
        v = kv_ref[:, width + hh * HEAD_DIM:width + (hh + 1) * HEAD_DIM]
        s = _dot_nt(q, k)
        p = jnp.exp(s - jnp.max(s, axis=-1, keepdims=True))
        o = _dot(p.astype(BF16), v) * (1.0 / jnp.sum(p, axis=-1, keepdims=True))
        z = z_ref[:, sl]
        o_ref[:, sl] = (o * (z * _sigmoid(z))).astype(o_ref.dtype)


def _mem_attention(h_tail, kv, batch, seq, n_mem, width, tq):
    nq = seq // tq
    body = functools.partial(_mem_body, width=width)
    return pl.pallas_call(
        body,
        grid=(batch, nq),
        in_specs=[pl.BlockSpec((tq, width), lambda b, i: (b * nq + i, 0)),
                  pl.BlockSpec((tq, width), lambda b, i: (b * nq + i, 1)),
                  pl.BlockSpec((n_mem, 2 * width), lambda b, i: (b, 0))],
        out_specs=pl.BlockSpec((tq, width), lambda b, i: (b * nq + i, 0)),
        out_shape=jax.ShapeDtypeStruct((batch * seq, width), BF16),
        compiler_params=_params(2),
        name="memory_attention",
    )(h_tail, h_tail, kv)


def _layer_tiles(m, d, seq, n_main, mem_rows, mem_cols):
    return dict(rows=_tile(m, 1024), in_cols=_tile(n_main, 1024), out_cols=_tile(d, 512), tail_rows=_tile(m, 512),
                mem_rows=_tile(mem_rows, 512), mem_cols=_tile(mem_cols, 1024), mem_queries=_tile(seq, 512),
                norm_rows=_tile(m, 256))


def _layer(x, mem, w_in, conv_w, a_log, dt_bias, dn_norm_w, w_mem_kv, w_gate, b_gate,
           w_up_sb, w_up_dn, w_up_mem, w_out, ln_g, ln_b, alpha):
    batch, seq, d = x.shape
    n_mem = mem.shape[1]
    m = batch * seq
    sbw, dnw, memw = w_up_sb.shape[0], w_up_dn.shape[0], w_up_mem.shape[0]
    sbh, dnh = sbw // HEAD_DIM, dnw // HEAD_DIM
    n_main = 4 * sbw + 4 * dnw
    ba0 = n_main
    qm0 = ba0 + 2 * dnh
    assert w_in.shape[1] == qm0 + 2 * memw

    x2 = x.reshape(m, d)
    tiles = _layer_tiles(m, d, seq, n_main, batch * n_mem, 2 * memw)
    tm = tiles["rows"]

    w_in_b = _to_bf16(w_in, _tile(d, 128))
    pad = (-2 * dnh) % HEAD_DIM
    w_tail = jnp.concatenate([w_in_b[:, qm0:], w_in_b[:, ba0:qm0], jnp.zeros((d, pad), BF16)], axis=1)
    h_tail, xb = _tail_projection(x2, w_tail, tiles["tail_rows"])
    col_scale = jnp.concatenate([jnp.full((sbw,), HEAD_DIM ** -0.5 * LOG2E, F32), jnp.ones((n_main - sbw,), F32)])
    h = _tiled_matmul(_proj_scale_body, xb, w_in_b,
                      [(col_scale.reshape(1, n_main), (1, tiles["in_cols"]), lambda j, i: (0, j))],
                      BF16, tm, tiles["in_cols"], "input_projection", n_cols=n_main)

    o_sb = _sb_attention(h, batch, seq, sbh, 0)

    ba = h_tail[:, 2 * memw:2 * memw + 2 * dnh].reshape(batch, seq, 2 * dnh)
    ba = ba.transpose(0, 2, 1).reshape(batch, 2 * dnh, seq // DN_CHUNK, DN_CHUNK)
    o_dn, gates = _deltanet_and_gates(h, conv_w, ba, a_log, dt_bias, dn_norm_w, xb, w_gate.astype(BF16), b_gate,
                                      batch, seq, dnh, 4 * sbh)

    kv = _tiled_matmul(_proj_body, mem.reshape(batch * n_mem, d).astype(BF16), w_mem_kv.astype(BF16), [], BF16,
                       tiles["mem_rows"], tiles["mem_cols"], "memory_kv_projection")
    o_m = _mem_attention(h_tail, kv, batch, seq, n_mem, memw, tiles["mem_queries"])

    y = _merge(o_sb, o_dn, o_m, w_up_sb.astype(BF16), w_up_dn.astype(BF16), w_up_mem.astype(BF16), gates, tm,
               tiles["out_cols"])
    r = _tiled_matmul(functools.partial(_resid_body, alpha=alpha), y, w_out.astype(BF16),
                      [(x2, (tm, tiles["out_cols"]), lambda j, i: (i, j))], F32, tm, tiles["out_cols"],
                      "output_projection")
    return _layernorm(r, ln_g, ln_b, tiles["norm_rows"]).reshape(batch, seq, d)


def kernel(x, mem, w_in, conv_w, a_log, dt_bias, dn_norm_w, w_mem_kv, w_gate, b_gate, w_up_sb, w_up_dn, w_up_mem,
           w_out, ln_g, ln_b):
    depth = w_in.shape[0]
    alpha = (2.0 * depth) ** 0.25
    for l in range(depth):
        x = _layer(x, mem, w_in[l], conv_w[l], a_log[l], dt_bias[l], dn_norm_w[l], w_mem_kv[l], w_gate[l],
                   b_gate[l], w_up_sb[l], w_up_dn[l], w_up_mem[l], w_out[l], ln_g[l], ln_b[l], alpha)
    return x
```

```python
import functools

import numpy as np
import jax
import jax.numpy as jnp
from jax import lax
from jax.experimental import pallas as pl
from jax.experimental.pallas import tpu as pltpu

HEAD_DIM = 128
CONV_K = 4
CONV_HALO = 8
LN_EPS = 1e-5
RMS_EPS = 1e-6
L2_EPS = 1e-6
LOG2E = 1.4426950408889634
SOFTPLUS2_CLAMP = 64.0
MASKED_LOG2_WEIGHT = -1e30
SB_TRI_EXTRA_ROWS = 16
SB_UNDERFLOW_LOG2 = 1100.0

DN_CHUNK = 128
DN_GROUP = 8
SB_BLOCK = 256
SB_HEADS_PER_STEP = 4
V7X_VMEM_LIMIT_BYTES = 56 * 1024 * 1024

F32 = jnp.float32
BF16 = jnp.bfloat16


def _params(n_axes):
    return pltpu.CompilerParams(dimension_semantics=("arbitrary",) * n_axes,
                                vmem_limit_bytes=V7X_VMEM_LIMIT_BYTES)


def _dot(a, b):
    return jnp.dot(a, b, preferred_element_type=F32)


def _dot_nt(a, b):
    return lax.dot_general(a, b, (((1,), (1,)), ((), ())), preferred_element_type=F32)


def _sigmoid(x):
    return 1.0 / (1.0 + jnp.exp(-x))


def _softplus(x):
    return jnp.maximum(x, 0.0) + jnp.log(1.0 + jnp.exp(-jnp.abs(x)))


def _softplus2(x):
    return jnp.maximum(x, jnp.log(1.0 + jnp.exp2(jnp.minimum(x, SOFTPLUS2_CLAMP))) * LOG2E)


def _tile(n, pref):
    t = min(n, pref)
    assert n % t == 0, (n, t)
    return t


def _tiled_matmul(body, a, b, extra, out_dtype, tm, tn, name, n_cols=None, single_buffer_b=False):
    m, k = a.shape
    n = b.shape[1] if n_cols is None else n_cols
    assert m % tm == 0 and n % tn == 0
    mode = dict(pipeline_mode=pl.Buffered(1)) if single_buffer_b else {}
    in_specs = [pl.BlockSpec((tm, k), lambda j, i: (i, 0)), pl.BlockSpec((k, tn), lambda j, i: (0, j), **mode)]
    in_specs += [pl.BlockSpec(blk, imap) for _, blk, imap in extra]
    return pl.pallas_call(
        body,
        grid=(n // tn, m // tm),
        in_specs=in_specs,
        out_specs=pl.BlockSpec((tm, tn), lambda j, i: (i, j)),
        out_shape=jax.ShapeDtypeStruct((m, n), out_dtype),
        compiler_params=_params(2),
        name=name,
    )(a, b, *[e[0] for e in extra])


def _proj_scale_body(a_ref, b_ref, s_ref, o_ref):
    o_ref[...] = (_dot(a_ref[...], b_ref[...]) * s_ref[...]).astype(o_ref.dtype)


def _tail_body(x_ref, w_ref, o_ref, xb_ref):
    xb = x_ref[...].astype(BF16)
    xb_ref[...] = xb
    o_ref[...] = _dot(xb, w_ref[...])


def _tail_projection(x2, w_tail, tm):
    m, d = x2.shape
    n = w_tail.shape[1]
    return pl.pallas_call(
        _tail_body,
        grid=(m // tm,),
        in_specs=[pl.BlockSpec((tm, d), lambda i: (i, 0)),
                  pl.BlockSpec((d, n), lambda i: (0, 0), pipeline_mode=pl.Buffered(1))],
        out_specs=[pl.BlockSpec((tm, n), lambda i: (i, 0)), pl.BlockSpec((tm, d), lambda i: (i, 0))],
        out_shape=[jax.ShapeDtypeStruct((m, n), F32), jax.ShapeDtypeStruct((m, d), BF16)],
        compiler_params=_params(1),
        name="input_projection_tail",
    )(x2, w_tail)


def _proj_body(a_ref, b_ref, o_ref):
    o_ref[...] = _dot(a_ref[...], b_ref[...]).astype(o_ref.dtype)


def _resid_body(a_ref, b_ref, x_ref, o_ref, *, alpha):
    o_ref[...] = alpha * x_ref[...] + _dot(a_ref[...], b_ref[...])


def _merge_body(osb_ref, odn_ref, om_ref, wsb_ref, wdn_ref, wm_ref, g0_ref, g1_ref, g2_ref, o_ref):
    y = g0_ref[...].astype(F32) * _dot(osb_ref[...], wsb_ref[...])
    y += g1_ref[...].astype(F32) * _dot(odn_ref[...], wdn_ref[...])
    y += g2_ref[...].astype(F32) * _dot(om_ref[...], wm_ref[...])
    o_ref[...] = y.astype(o_ref.dtype)


def _merge(o_sb, o_dn, o_m, w_sb, w_dn, w_m, gates, tm, tn):
    m = o_sb.shape[0]
    d = w_sb.shape[1]
    row = lambda w: pl.BlockSpec((tm, w), lambda j, i: (i, 0))
    colw = lambda w: pl.BlockSpec((w, tn), lambda j, i: (0, j), pipeline_mode=pl.Buffered(1))
    gate = pl.BlockSpec((tm, tn), lambda j, i: (i, j))
    return pl.pallas_call(
        _merge_body,
        grid=(d // tn, m // tm),
        in_specs=[row(o_sb.shape[1]), row(o_dn.shape[1]), row(o_m.shape[1]),
                  colw(w_sb.shape[0]), colw(w_dn.shape[0]), colw(w_m.shape[0]),
                  gate, gate, gate],
        out_specs=pl.BlockSpec((tm, tn), lambda j, i: (i, j)),
        out_shape=jax.ShapeDtypeStruct((m, d), BF16),
        compiler_params=_params(2),
        name="branch_merge",
    )(o_sb, o_dn, o_m, w_sb, w_dn, w_m, *gates)


def _layernorm_body(r_ref, g_ref, b_ref, o_ref):
    r = r_ref[...]
    mu = jnp.mean(r, axis=-1, keepdims=True)
    rc = r - mu
    var = jnp.mean(rc * rc, axis=-1, keepdims=True)
    o_ref[...] = rc * lax.rsqrt(var + LN_EPS) * g_ref[...] + b_ref[...]


def _layernorm(r, g, b, tm):
    m, d = r.shape
    return pl.pallas_call(
        _layernorm_body,
        grid=(m // tm,),
        in_specs=[pl.BlockSpec((tm, d), lambda i: (i, 0)),
                  pl.BlockSpec((1, d), lambda i: (0, 0)),
                  pl.BlockSpec((1, d), lambda i: (0, 0))],
        out_specs=pl.BlockSpec((tm, d), lambda i: (i, 0)),
        out_shape=jax.ShapeDtypeStruct((m, d), F32),
        compiler_params=_params(1),
        name="post_layernorm",
    )(r, g.reshape(1, d), b.reshape(1, d))


def _sb_body(q_ref, k_ref, v_ref, z_ref, tri_ref, o_ref, vt_ref, acc_ref, lsig_ref, sp_ref, *, tk, nq, hps):
    qi = pl.program_id(2)
    heads = range(hps)
    lanes = lambda hh: slice(hh * HEAD_DIM, (hh + 1) * HEAD_DIM)

    @pl.when(qi == 0)
    def _():
        def transpose_block(c, _):
            blk = v_ref[pl.ds(pl.multiple_of(c * tk, tk), tk), :].astype(F32)
            for hh in heads:
                vt_ref[hh, c] = blk[:, lanes(hh)].T.astype(BF16)
            return 0
        lax.fori_loop(0, nq, transpose_block, 0)

    qs = [q_ref[:, lanes(hh)] for hh in heads]
    key_pos = lax.broadcasted_iota(jnp.int32, (tk, tk), 0)
    qry_pos = lax.broadcasted_iota(jnp.int32, (tk, tk), 1)
    before = key_pos < qry_pos

    def score_dots(j):
        kb = k_ref[pl.ds(pl.multiple_of(j * tk, tk), tk), :]
        return [_dot_nt(kb[:, lanes(hh)], qs[hh]) for hh in heads]

    def scores_store(zts, slot, diagonal):
        for hh in heads:
            sp = _softplus2(zts[hh])
            lsig = zts[hh] - sp
            if diagonal:
                sp = jnp.where(before, sp, 0.0)
                lsig = jnp.where(before, lsig, MASKED_LOG2_WEIGHT)
            lsig_ref[slot, hh] = lsig
            sp_ref[slot, hh] = sp.astype(BF16)

    def suffix_dots(slot):
        return [_dot(tri_ref[...], sp_ref[slot, hh]) for hh in heads]

    def finish(j, slot, sufs, carries):
        out = []
        for hh in heads:
            w = jnp.exp2(lsig_ref[slot, hh] - sufs[hh][:tk] - carries[hh])
            acc_ref[hh] += _dot(vt_ref[hh, j], w.astype(BF16))
            out.append(carries[hh] + sufs[hh][tk:tk + 1])
        return tuple(out)

    def step(state):
        t, carries, _ = state
        slot = t & 1
        sufs = suffix_dots(slot)
        zts = score_dots(qi - t - 1)
        carries = finish(qi - t, slot, sufs, carries)
        scores_store(zts, 1 - slot, False)
        smallest = functools.reduce(jnp.minimum, carries)
        return t + 1, carries, jnp.min(smallest) < SB_UNDERFLOW_LOG2

    acc_ref[...] = jnp.zeros(acc_ref.shape, F32)
    scores_store(score_dots(qi), 0, True)
    t, carries, live = lax.while_loop(lambda s: jnp.logical_and(s[0] < qi, s[2]), step,
                                      (jnp.int32(0), (jnp.zeros((1, tk), F32),) * hps, jnp.bool_(True)))

    @pl.when(live)
    def _():
        last = t & 1
        finish(qi - t, last, suffix_dots(last), carries)

    for hh in heads:
        z = z_ref[:, lanes(hh)].astype(F32)
        o_ref[:, lanes(hh)] = (acc_ref[hh].T * (z * _sigmoid(z))).astype(o_ref.dtype)


def _sb_attention(h, batch, seq, heads, col0):
    t = _tile(seq, SB_BLOCK)
    nq = seq // t
    hps = SB_HEADS_PER_STEP if heads % SB_HEADS_PER_STEP == 0 and col0 % SB_HEADS_PER_STEP == 0 else 1
    ng = heads // hps
    c0 = col0 // hps
    w = hps * HEAD_DIM
    tri = np.zeros((t + SB_TRI_EXTRA_ROWS, t), np.float32)
    tri[:t] = np.triu(np.ones((t, t), np.float32), 1)
    tri[t] = 1.0
    tri = jnp.asarray(tri, BF16)
    body = functools.partial(_sb_body, tk=t, nq=nq, hps=hps)
    whole_seq = lambda g: pl.BlockSpec((seq, w), lambda b, h_, i: (b, c0 + g * ng + h_), pipeline_mode=pl.Buffered(1))
    return pl.pallas_call(
        body,
        grid=(batch, ng, nq),
        in_specs=[pl.BlockSpec((t, w), lambda b, h_, i: (b * nq + i, c0 + h_)),
                  whole_seq(1), whole_seq(2),
                  pl.BlockSpec((t, w), lambda b, h_, i: (b * nq + i, c0 + 3 * ng + h_)),
                  pl.BlockSpec((t + SB_TRI_EXTRA_ROWS, t), lambda b, h_, i: (0, 0))],
        out_specs=pl.BlockSpec((t, w), lambda b, h_, i: (b * nq + i, h_)),
        out_shape=jax.ShapeDtypeStruct((batch * seq, heads * HEAD_DIM), BF16),
        scratch_shapes=[pltpu.VMEM((hps, nq, HEAD_DIM, t), BF16), pltpu.VMEM((hps, HEAD_DIM, t), F32),
                        pltpu.VMEM((2, hps, t, t), F32), pltpu.VMEM((2, hps, t, t), BF16)],
        compiler_params=_params(3),
        name="stickbreak_attention",
    )(h, h, h, h, tri)


_M_INCL, _M_STRICT, _M_PAIR, _M_LEVEL0 = 0, 1, 2, 3
_N_LEVELS = int(np.log2(DN_CHUNK)) - 1
_M_EYE = _M_LEVEL0 + _N_LEVELS
_M_UPPER = _M_EYE + 1
_N_MASKS = _M_UPPER + 1
_DN_STAGES = 2 * _N_LEVELS + 4
GATE_K_PIECES = 4
assert _DN_STAGES % GATE_K_PIECES == 0


def _dn_masks():
    c = DN_CHUNK
    i = np.arange(c)[:, None]
    j = np.arange(c)[None, :]
    masks = [i >= j, i > j, (i % 2 == 1) & (j == i - 1)]
    b = 4
    while b <= c:
        masks.append((i // b == j // b) & (i % b >= b // 2) & (j % b < b // 2))
        b *= 2
    masks += [i == j, i <= j]
    assert len(masks) == _N_MASKS
    return np.stack(masks).astype(np.float32)


def _split3(x):
    p1 = x.astype(BF16)
    r = x - p1.astype(F32)
    p2 = r.astype(BF16)
    p3 = (r - p2.astype(F32)).astype(BF16)
    return p1, p2, p3


def _dn_body(q_ref, k_ref, v_ref, z_ref, cq_ref, ck_ref, cv_ref, b_ref, a_ref, alog_ref, dtb_ref, nw_ref, msk_ref,
             x_ref, wg0_ref, wg1_ref, wg2_ref, bg0_ref, bg1_ref, bg2_ref,
             o_ref, g0_ref, g1_ref, g2_ref,
             gc_ref, beta_ref, gl_ref, xf_ref, lhs_ref, add_ref, state_ref, acc_ref, *, n_groups):
    c = DN_CHUNK
    rows = DN_GROUP * c
    g = pl.program_id(2)
    wg_refs, bg_refs, gate_refs = (wg0_ref, wg1_ref, wg2_ref), (bg0_ref, bg1_ref, bg2_ref), (g0_ref, g1_ref, g2_ref)
    k_piece = x_ref.shape[1] // GATE_K_PIECES

    @pl.when(g == 0)
    def _():
        gd = -jnp.exp(alog_ref[0]) * _softplus(a_ref[0, 0] + dtb_ref[0])
        upper = msk_ref[_M_UPPER].astype(BF16)
        ones = jnp.ones((c, c), BF16)
        pieces = _split3(gd)
        gc_ref[...] = sum(_dot(p, upper) for p in pieces)
        gl_ref[...] = sum(_dot(p, ones) for p in pieces)
        beta_ref[...] = _sigmoid(b_ref[0, 0])
        state_ref[...] = jnp.zeros(state_ref.shape, F32)

    def gate_piece(i):
        ks = slice(i * k_piece, (i + 1) * k_piece)
        for t in range(3):
            part = _dot(x_ref[:, ks], wg_refs[t][ks, :])
            if i == 0:
                acc_ref[t] = part
            elif i < GATE_K_PIECES - 1:
                acc_ref[t] += part
            else:
                gate_refs[t][...] = _sigmoid(acc_ref[t] + part + bg_refs[t][...]).astype(gate_refs[t].dtype)

    def conv_silu(idx, x_blk_ref, cw_ref, first):
        if first:
            xf_ref[idx, 0:CONV_HALO, :] = jnp.zeros((CONV_HALO, HEAD_DIM), F32)
        else:
            xf_ref[idx, 0:CONV_HALO, :] = xf_ref[idx, rows:rows + CONV_HALO, :]
        xf_ref[idx, CONV_HALO:, :] = x_blk_ref[...].astype(F32)
        cw = cw_ref[...]
        y = None
        for s in range(CONV_K):
            tap = xf_ref[idx, CONV_HALO - s:CONV_HALO - s + rows, :] * cw[CONV_K - 1 - s:CONV_K - s, :]
            y = tap if y is None else y + tap
        return y * _sigmoid(y)

    def l2norm(t):
        return t * lax.rsqrt(jnp.sum(t * t, axis=-1, keepdims=True) + L2_EPS)

    def local_work(first, per_stage):
        stage = [0]

        def stage_done():
            for fn in per_stage[stage[0]]:
                fn()
            stage[0] += 1

        slot = g & 1
        qg = l2norm(conv_silu(0, q_ref, cq_ref, first)) * (HEAD_DIM ** -0.5)
        kg = l2norm(conv_silu(1, k_ref, ck_ref, first))
        vg = conv_silu(2, v_ref, cv_ref, first)
        cs = range(DN_GROUP)
        sl = lambda ci: slice(ci * c, (ci + 1) * c)
        qs, ks, vs = [qg[sl(ci)] for ci in cs], [kg[sl(ci)] for ci in cs], [vg[sl(ci)] for ci in cs]
        gls, gc_cols, beta_cols, decays = [], [], [], []
        for ci in cs:
            chunk = g * DN_GROUP + ci
            gc_row = jnp.broadcast_to(gc_ref[pl.ds(chunk, 1), :], (c, c))
            gc_col = gc_row.T
            incl = msk_ref[_M_INCL]
            gls.append(gl_ref[pl.ds(chunk, 1), :])
            gc_cols.append(gc_col)
            beta_cols.append(jnp.broadcast_to(beta_ref[pl.ds(chunk, 1), :], (c, c)).T)
            decays.append(jnp.exp((gc_col - gc_row) * incl) * incl)
        kbs = [ks[ci] * beta_cols[ci] for ci in cs]
        prods = [_dot_nt(jnp.concatenate([qs[ci], kbs[ci]], axis=0).astype(BF16), ks[ci].astype(BF16)) for ci in cs]
        stage_done()
        a_ins = [prods[ci][:c] * decays[ci] for ci in cs]
        a_los = [prods[ci][c:] * decays[ci] * msk_ref[_M_STRICT] for ci in cs]
        t_invs = [msk_ref[_M_EYE] - a_lo * msk_ref[_M_PAIR] for a_lo in a_los]
        for lvl in range(_N_LEVELS):
            offs = [(a_lo * msk_ref[_M_LEVEL0 + lvl]).astype(BF16) for a_lo in a_los]
            tbs = [t_inv.astype(BF16) for t_inv in t_invs]
            xs = [_dot(tbs[ci], offs[ci]).astype(BF16) for ci in cs]
            stage_done()
            t_invs = [t_invs[ci] - _dot(xs[ci], tbs[ci]) for ci in cs]
            stage_done()
        eg_cols = [jnp.exp(gc_col) for gc_col in gc_cols]
        rhss = [jnp.concatenate([kbs[ci] * eg_cols[ci], vs[ci] * beta_cols[ci]], axis=1).astype(BF16) for ci in cs]
        wus = [_dot(t_invs[ci].astype(BF16), rhss[ci]).astype(BF16) for ci in cs]
        stage_done()
        k_ends = [ks[ci] * jnp.exp(gls[ci] - gc_cols[ci]) for ci in cs]
        pns = [_dot(k_ends[ci].T.astype(BF16), wus[ci]) for ci in cs]
        stage_done()
        aw_aus = [_dot(a_ins[ci].astype(BF16), wus[ci]) for ci in cs]
        stage_done()
        assert stage[0] == _DN_STAGES == len(per_stage)
        for ci in cs:
            q_eff = qs[ci] * eg_cols[ci] - aw_aus[ci][:, :HEAD_DIM]
            lhs_ref[slot, ci] = jnp.concatenate([q_eff, pns[ci][:, :HEAD_DIM]], axis=0).astype(BF16)
            add_ref[slot, ci] = jnp.concatenate([aw_aus[ci][:, HEAD_DIM:], pns[ci][:, HEAD_DIM:]], axis=0)

    def scan_chunk(slot, ci, state):
        both = _dot(lhs_ref[slot, ci], state.astype(BF16))
        o = both[:c] + add_ref[slot, ci, :c]
        gl = gl_ref[pl.ds((g - 1) * DN_GROUP + ci, 1), :]
        state = state * jnp.exp(gl) - both[c:] + add_ref[slot, ci, c:]
        o = o * lax.rsqrt(jnp.mean(o * o, axis=-1, keepdims=True) + RMS_EPS) * nw_ref[...]
        zr = z_ref[ci * c:(ci + 1) * c, :].astype(F32)
        o_ref[ci * c:(ci + 1) * c, :] = (o * (zr * _sigmoid(zr))).astype(o_ref.dtype)
        return state

    def run(first, with_scan):
        box = [state_ref[...]] if with_scan else None

        def scan_hook(ci):
            box[0] = scan_chunk(1 - (g & 1), ci, box[0])

        per_stage = [[] for _ in range(_DN_STAGES)]
        for i in range(GATE_K_PIECES):
            per_stage[i * (_DN_STAGES // GATE_K_PIECES)].append(functools.partial(gate_piece, i))
        if with_scan:
            for ci in range(DN_GROUP):
                per_stage[2 * ci + 1].append(functools.partial(scan_hook, ci))
        local_work(first, per_stage)
        if with_scan:
            state_ref[...] = box[0]

    @pl.when(g == 0)
    def _():
        run(True, False)

    @pl.when(jnp.logical_and(g > 0, g < n_groups))
    def _():
        run(False, True)

    @pl.when(g == n_groups)
    def _():
        state = state_ref[...]
        for ci in range(DN_GROUP):
            state = scan_chunk((n_groups - 1) & 1, ci, state)


def _deltanet_and_gates(h, conv_w, ba, a_log, dt_bias, norm_w, xb, w_gate, b_gate, batch, seq, heads, col0):
    c = DN_CHUNK
    rows = DN_GROUP * c
    assert seq % rows == 0 and w_gate.shape[0] == 3
    n_chunks, ng = seq // c, seq // rows
    m, d = xb.shape
    gw = d // heads
    assert gw * heads == d and gw % HEAD_DIM == 0 and d % GATE_K_PIECES == 0
    masks = jnp.asarray(_dn_masks())
    lane = lambda v: jnp.broadcast_to(v.astype(F32)[:, None, None], (heads, 1, c))
    body = functools.partial(_dn_body, n_groups=ng)
    cur = lambda b, g: b * ng + jnp.minimum(g, ng - 1)
    prev = lambda b, g: b * ng + jnp.maximum(g - 1, 0)
    hcol = lambda t: pl.BlockSpec((rows, HEAD_DIM), lambda b, h_, g: (cur(b, g), col0 + t * heads + h_))
    cw = lambda t: pl.BlockSpec((CONV_K, HEAD_DIM), lambda b, h_, g: (0, t * heads + h_))
    wgs = lambda t: pl.BlockSpec((None, d, gw), lambda b, h_, g: (t, 0, h_))
    bgs = lambda t: pl.BlockSpec((None, 1, gw), lambda b, h_, g: (t, 0, h_))
    b_gate = b_gate.reshape(3, 1, d)
    gate_out = pl.BlockSpec((rows, gw), lambda b, h_, g: (cur(b, g), h_))
    outs = pl.pallas_call(
        body,
        grid=(batch, heads, ng + 1),
        in_specs=[hcol(0), hcol(1), hcol(2),
                  pl.BlockSpec((rows, HEAD_DIM), lambda b, h_, g: (prev(b, g), col0 + 3 * heads + h_)),
                  cw(0), cw(1), cw(2),
                  pl.BlockSpec((1, 1, n_chunks, c), lambda b, h_, g: (b, h_, 0, 0)),
                  pl.BlockSpec((1, 1, n_chunks, c), lambda b, h_, g: (b, heads + h_, 0, 0)),
                  pl.BlockSpec((1, 1, c), lambda b, h_, g: (h_, 0, 0)),
                  pl.BlockSpec((1, 1, c), lambda b, h_, g: (h_, 0, 0)),
                  pl.BlockSpec((1, HEAD_DIM), lambda b, h_, g: (0, 0)),
                  pl.BlockSpec((_N_MASKS, c, c), lambda b, h_, g: (0, 0, 0)),
                  pl.BlockSpec((rows, d), lambda b, h_, g: (cur(b, g), 0)),
                  wgs(0), wgs(1), wgs(2), bgs(0), bgs(1), bgs(2)],
        out_specs=[pl.BlockSpec((rows, HEAD_DIM), lambda b, h_, g: (prev(b, g), h_)), gate_out, gate_out, gate_out],
        out_shape=[jax.ShapeDtypeStruct((m, heads * HEAD_DIM), BF16)] + [jax.ShapeDtypeStruct((m, d), BF16)] * 3,
        scratch_shapes=[pltpu.VMEM((n_chunks, c), F32)] * 3
        + [pltpu.VMEM((3, CONV_HALO + rows, HEAD_DIM), F32),
           pltpu.VMEM((2, DN_GROUP, 2 * c, HEAD_DIM), BF16), pltpu.VMEM((2, DN_GROUP, 2 * c, HEAD_DIM), F32),
           pltpu.VMEM((HEAD_DIM, HEAD_DIM), F32), pltpu.VMEM((3, rows, gw), F32)],
        compiler_params=_params(3),
        name="deltanet_and_gates",
    )(h, h, h, h, conv_w, conv_w, conv_w, ba, ba, lane(a_log), lane(dt_bias), norm_w.reshape(1, HEAD_DIM), masks,
      xb, w_gate, w_gate, w_gate, b_gate, b_gate, b_gate)
    return outs[0], outs[1:]


def _mem_body(q_ref, z_ref, kv_ref, o_ref, *, width):
    for hh in range(width // HEAD_DIM):
        sl = slice(hh * HEAD_DIM, (hh + 1) * HEAD_DIM)
        q = (q_ref[:, sl] * (HEAD_DIM ** -0.5)).astype(BF16)
        k = kv_ref[:, sl]
        v = kv_ref[:, width + hh * HEAD_DIM:width + (hh + 1) * HEAD_DIM]
        s = _dot_nt(q, k)
        p = jnp.exp(s - jnp.max(s, axis=-1, keepdims=True))
        o = _dot(p.astype(BF16), v) * (1.0 / jnp.sum(p, axis=-1, keepdims=True))
        z = z_ref[:, sl]
        o_ref[:, sl] = (o * (z * _sigmoid(z))).astype(o_ref.dtype)


def _mem_attention(h_tail, kv, batch, seq, n_mem, width, tq):
    nq = seq // tq
    body = functools.partial(_mem_body, width=width)
    return pl.pallas_call(
        body,
        grid=(batch, nq),
        in_specs=[pl.BlockSpec((tq, width), lambda b, i: (b * nq + i, 0)),
                  pl.BlockSpec((tq, width), lambda b, i: (b * nq + i, 1)),
                  pl.BlockSpec((n_mem, 2 * width), lambda b, i: (b, 0))],
        out_specs=pl.BlockSpec((tq, width), lambda b, i: (b * nq + i, 0)),
        out_shape=jax.ShapeDtypeStruct((batch * seq, width), BF16),
        compiler_params=_params(2),
        name="memory_attention",
    )(h_tail, h_tail, kv)


def _layer_tiles(m, d, seq, n_main, mem_rows, mem_cols):
    return dict(rows=_tile(m, 1024), in_cols=_tile(n_main, 1024), out_cols=_tile(d, 1024), tail_rows=_tile(m, 512),
                mem_rows=_tile(mem_rows, 512), mem_cols=_tile(mem_cols, 1024), mem_queries=_tile(seq, 512),
                norm_rows=_tile(m, 256))


def _layer(x, mem, w_in, conv_w, a_log, dt_bias, dn_norm_w, w_mem_kv, w_gate, b_gate,
           w_up_sb, w_up_dn, w_up_mem, w_out, ln_g, ln_b, alpha):
    batch, seq, d = x.shape
    n_mem = mem.shape[1]
    m = batch * seq
    sbw, dnw, memw = w_up_sb.shape[0], w_up_dn.shape[0], w_up_mem.shape[0]
    sbh, dnh = sbw // HEAD_DIM, dnw // HEAD_DIM
    n_main = 4 * sbw + 4 * dnw
    ba0 = n_main
    qm0 = ba0 + 2 * dnh
    assert w_in.shape[1] == qm0 + 2 * memw

    x2 = x.reshape(m, d)
    tiles = _layer_tiles(m, d, seq, n_main, batch * n_mem, 2 * memw)
    tm = tiles["rows"]

    w_in_b = w_in.astype(BF16)
    pad = (-2 * dnh) % HEAD_DIM
    w_tail = jnp.concatenate([w_in_b[:, qm0:], w_in_b[:, ba0:qm0], jnp.zeros((d, pad), BF16)], axis=1)
    h_tail, xb = _tail_projection(x2, w_tail, tiles["tail_rows"])
    col_scale = jnp.concatenate([jnp.full((sbw,), HEAD_DIM ** -0.5 * LOG2E, F32), jnp.ones((n_main - sbw,), F32)])
    h = _tiled_matmul(_proj_scale_body, xb, w_in_b,
                      [(col_scale.reshape(1, n_main), (1, tiles["in_cols"]), lambda j, i: (0, j))],
                      BF16, tm, tiles["in_cols"], "input_projection", n_cols=n_main)

    o_sb = _sb_attention(h, batch, seq, sbh, 0)

    ba = h_tail[:, 2 * memw:2 * memw + 2 * dnh].reshape(batch, seq, 2 * dnh)
    ba = ba.transpose(0, 2, 1).reshape(batch, 2 * dnh, seq // DN_CHUNK, DN_CHUNK)
    o_dn, gates = _deltanet_and_gates(h, conv_w, ba, a_log, dt_bias, dn_norm_w, xb, w_gate.astype(BF16), b_gate,
                                      batch, seq, dnh, 4 * sbh)

    kv = _tiled_matmul(_proj_body, mem.reshape(batch * n_mem, d).astype(BF16), w_mem_kv.astype(BF16), [], BF16,
                       tiles["mem_rows"], tiles["mem_cols"], "memory_kv_projection")
    o_m = _mem_attention(h_tail, kv, batch, seq, n_mem, memw, tiles["mem_queries"])

    y = _merge(o_sb, o_dn, o_m, w_up_sb.astype(BF16), w_up_dn.astype(BF16), w_up_mem.astype(BF16), gates, tm,
               tiles["out_cols"])
    r = _tiled_matmul(functools.partial(_resid_body, alpha=alpha), y, w_out.astype(BF16),
                      [(x2, (tm, tiles["out_cols"]), lambda j, i: (i, j))], F32, tm, tiles["out_cols"],
                      "output_projection", single_buffer_b=True)
    return _layernorm(r, ln_g, ln_b, tiles["norm_rows"]).reshape(batch, seq, d)


def kernel(x, mem, w_in, conv_w, a_log, dt_bias, dn_norm_w, w_mem_kv, w_gate, b_gate, w_up_sb, w_up_dn, w_up_mem,
           w_out, ln_g, ln_b):
    depth = w_in.shape[0]
    alpha = (2.0 * depth) ** 0.25
    for l in range(depth):
        x = _layer(x, mem, w_in[l], conv_w[l], a_log[l], dt_bias[l], dn_norm_w[l], w_mem_kv[l], w_gate[l],
                   b_gate[l], w_up_sb[l], w_up_dn[l], w_up_mem[l], w_out[l], ln_g[l], ln_b[l], alpha)
    return x
```
